```python
import jax, jax.numpy as jnp
from jax import lax
import numpy as np

D_MODEL = 2048
BATCH = 2
SEQ = 4096
DEPTH = 2

N_MIXERS = 2
N_CONV_LAYERS = (DEPTH + 1) // 2
N_RWKV_LAYERS = DEPTH // 2
CONV_WIDTH = 3
HEAD_SIZE = 64
N_HEADS = D_MODEL // HEAD_SIZE
D_DECAY_LORA = 96
D_AAA_LORA = 96
D_GATE_LORA = 256
D_FF = 5632
RMS_EPS = 1e-6
GN_EPS = 64e-5
L2_EPS = 1e-12

kernel_name = "hybrid_shortconv_rwkv7_convffn"


def rmsnorm(x, g):
    xf = x.astype(jnp.float32)
    y = xf * lax.rsqrt(jnp.mean(xf * xf, axis=-1, keepdims=True) + RMS_EPS)
    return (y * g.astype(jnp.float32)).astype(x.dtype)


def causal_dwconv(x, w):
    c = x.shape[-1]
    return lax.conv_general_dilated(
        x, w[:, None, :].astype(x.dtype), window_strides=(1,),
        padding=[(w.shape[0] - 1, 0)],
        dimension_numbers=("NWC", "WIO", "NWC"), feature_group_count=c)


def short_conv_mixer(x, w_in, conv_w, w_out):
    b, c, h = jnp.split(x @ w_in, 3, axis=-1)
    return (b * causal_dwconv(c * h, conv_w)) @ w_out


def rwkv7_time_mix(x, mu, w_r, w_k, w_v, w_o, w0, w1, w2, a0, a1, a2,
                   g1, g2, k_k, k_a, r_k, gn_g, gn_b):
    bsz, t, d = x.shape
    h, n = N_HEADS, HEAD_SIZE
    xx = jnp.pad(x, ((0, 0), (1, 0), (0, 0)))[:, :-1] - x
    xm = x[None] + xx[None] * mu[:, None, None, :]
    xr, xw, xk, xv, xa, xg = xm[0], xm[1], xm[2], xm[3], xm[4], xm[5]

    r = xr @ w_r
    k = xk @ w_k
    v = xv @ w_v
    w_log = -jax.nn.softplus(-(w0 + jnp.tanh(xw @ w1) @ w2)) - 0.5
    decay = jnp.exp(-jnp.exp(w_log.astype(jnp.float32)))
    a = jax.nn.sigmoid(a0 + (xa @ a1) @ a2)
    g = jax.nn.sigmoid(xg @ g1) @ g2

    kk = (k * k_k).reshape(bsz, t, h, n).astype(jnp.float32)
    kk = kk / jnp.maximum(jnp.linalg.norm(kk, axis=-1, keepdims=True), L2_EPS)
    k = k * (1.0 + (a - 1.0) * k_a)

    def heads(z):
        return z.reshape(bsz, t, h, n).astype(jnp.float32)

    rh, kh, vh, ah = heads(r), heads(k), heads(v), heads(a)
    wh = decay.reshape(bsz, t, h, n)
    a_vec = -kk
    b_vec = kk * ah

    def step(S, inp):
        r_t, w_t, k_t, v_t, av_t, bv_t = inp
        sa = jnp.einsum('bhvk,bhk->bhv', S, av_t)
        S = (S * w_t[:, :, None, :] + sa[..., None] * bv_t[:, :, None, :]
             + v_t[..., None] * k_t[:, :, None, :])
        y_t = jnp.einsum('bhvk,bhk->bhv', S, r_t)
        return S, y_t

    seq_first = lambda z: jnp.swapaxes(z, 0, 1)
    S0 = jnp.zeros((bsz, h, n, n), jnp.float32)
    _, y = lax.scan(step, S0, (seq_first(rh), seq_first(wh), seq_first(kh),
                               seq_first(vh), seq_first(a_vec), seq_first(b_vec)))
    y = jnp.swapaxes(y, 0, 1)

    mean = jnp.mean(y, axis=-1, keepdims=True)
    var = jnp.mean(jnp.square(y - mean), axis=-1, keepdims=True)
    y = ((y - mean) * lax.rsqrt(var + GN_EPS)).reshape(bsz, t, d)
    y = y * gn_g.astype(jnp.float32) + gn_b.astype(jnp.float32)
    bonus = jnp.sum(rh * kh * r_k.astype(jnp.float32), axis=-1, keepdims=True) * vh
    y = (y + bonus.reshape(bsz, t, d)).astype(x.dtype)
    return (y * g) @ w_o


def conv_ffn(x, w_up, conv_w, conv_b, w_down):
    u = causal_dwconv(x @ w_up, conv_w) + conv_b
    gate, up = jnp.split(u, 2, axis=-1)
    return (jax.nn.silu(gate) * up) @ w_down


def setup_inputs(seed: int = 0) -> dict:
    key = jax.random.key(seed)
    ks = iter(jax.random.split(key, 64))
    D, F, H, N = D_MODEL, D_FF, N_HEADS, HEAD_SIZE
    NC, NR = N_CONV_LAYERS, N_RWKV_LAYERS

    def nrm(shape, fan_in, scale=1.0):
        return jax.random.normal(next(ks), shape, jnp.float32) * (scale * fan_in ** -0.5)

    def unif(shape, lo, hi):
        return jax.random.uniform(next(ks), shape, jnp.float32, lo, hi)

    def near(shape, center, s):
        return center + s * jax.random.normal(next(ks), shape, jnp.float32)

    return {
        "x": jax.random.normal(next(ks), (BATCH, SEQ, D), jnp.float32),
        "norm_g": near((DEPTH, 4, D), 1.0, 0.02),
        "sc_w_in": nrm((NC, D, 3 * D), D),
        "sc_conv": nrm((NC, CONV_WIDTH, D), CONV_WIDTH),
        "sc_w_out": nrm((NC, D, D), D),
        "rw_mu": unif((NR, 6, D), 0.0, 1.0),
        "rw_wr": nrm((NR, D, D), D),
        "rw_wk": nrm((NR, D, D), D),
        "rw_wv": nrm((NR, D, D), D),
        "rw_wo": nrm((NR, D, D), D),
        "rw_w0": unif((NR, D), -6.0, -1.0),
        "rw_w1": nrm((NR, D, D_DECAY_LORA), D),
        "rw_w2": nrm((NR, D_DECAY_LORA, D), D_DECAY_LORA, 0.1),
        "rw_a0": near((NR, D), 0.0, 0.1),
        "rw_a1": nrm((NR, D, D_AAA_LORA), D),
        "rw_a2": nrm((NR, D_AAA_LORA, D), D_AAA_LORA, 0.1),
        "rw_g1": nrm((NR, D, D_GATE_LORA), D),
        "rw_g2": nrm((NR, D_GATE_LORA, D), D_GATE_LORA),
        "rw_kk": near((NR, D), 0.85, 0.02),
        "rw_ka": near((NR, D), 1.0, 0.02),
        "rw_rk": near((NR, H, N), 0.0, 0.1),
        "rw_gn_g": near((NR, D), 1.0, 0.02),
        "rw_gn_b": near((NR, D), 0.0, 0.02),
        "ffn_w_up": nrm((DEPTH, D, 2 * F), D),
        "ffn_conv": nrm((DEPTH, CONV_WIDTH, 2 * F), CONV_WIDTH),
        "ffn_conv_b": near((DEPTH, 2 * F), 0.0, 0.02),
        "ffn_w_down": nrm((DEPTH, F, D), F),
    }


def reference(x, norm_g, sc_w_in, sc_conv, sc_w_out,
              rw_mu, rw_wr, rw_wk, rw_wv, rw_wo, rw_w0, rw_w1, rw_w2,
              rw_a0, rw_a1, rw_a2, rw_g1, rw_g2, rw_kk, rw_ka, rw_rk,
              rw_gn_g, rw_gn_b,
              ffn_w_up, ffn_conv, ffn_conv_b, ffn_w_down):
    for i in range(DEPTH):
        j = i // N_MIXERS
        h = rmsnorm(x, norm_g[i, 0])
        if i % N_MIXERS == 0:
            h = short_conv_mixer(h, sc_w_in[j], sc_conv[j], sc_w_out[j])
        else:
            h = rwkv7_time_mix(h, rw_mu[j], rw_wr[j], rw_wk[j], rw_wv[j], rw_wo[j],
                               rw_w0[j], rw_w1[j], rw_w2[j], rw_a0[j], rw_a1[j], rw_a2[j],
                               rw_g1[j], rw_g2[j], rw_kk[j], rw_ka[j], rw_rk[j],
                               rw_gn_g[j], rw_gn_b[j])
        x = x + rmsnorm(h, norm_g[i, 1])
        h = rmsnorm(x, norm_g[i, 2])
        h = conv_ffn(h, ffn_w_up[i], ffn_conv[i], ffn_conv_b[i], ffn_w_down[i])
        x = x + rmsnorm(h, norm_g[i, 3])
    return x
```

```python
import functools

import jax
import jax.numpy as jnp
from jax import lax
from jax.experimental import pallas as pl
from jax.experimental.pallas import tpu as pltpu

F32 = jnp.float32
BF16 = jnp.bfloat16

HEAD_SIZE = 64
RMS_EPS = 1e-6
GN_EPS = 64e-5
L2_EPS = 1e-12

LANES = 128
BF16_SUBLANES = 16
VMEM_LIMIT_BYTES = 56 * 1024 * 1024
CHUNK = 64
PAIR = 2 * HEAD_SIZE


def _params(n_axes):
    return pltpu.CompilerParams(dimension_semantics=("arbitrary",) * n_axes,
                                vmem_limit_bytes=VMEM_LIMIT_BYTES)


def _dot(a, b):
    return jnp.dot(a, b, preferred_element_type=F32)


def _dot_nt(a, b):
    return lax.dot_general(a, b, (((1,), (1,)), ((), ())), preferred_element_type=F32)


def _dot_tn(a, b):
    return lax.dot_general(a, b, (((0,), (0,)), ((), ())), preferred_element_type=F32)


def _split2(x):
    hi = x.astype(BF16)
    lo = (x - hi.astype(F32)).astype(BF16)
    return hi, lo


def _split3(x):
    hi = x.astype(BF16)
    r1 = x - hi.astype(F32)
    mid = r1.astype(BF16)
    lo = (r1 - mid.astype(F32)).astype(BF16)
    return hi, mid, lo


def _rms(y, g):
    return y * lax.rsqrt(jnp.mean(y * y, axis=-1, keepdims=True) + RMS_EPS) * g


def _sigmoid(x):
    return 1.0 / (1.0 + jnp.exp(-x))


def _head_sum(x, bd):
    hi, lo = _split2(x)
    cols = []
    for s in range(x.shape[1] // LANES):
        sl = slice(s * LANES, (s + 1) * LANES)
        cols.append(_dot(hi[:, sl], bd) + _dot(lo[:, sl], bd))
    return cols[0] if len(cols) == 1 else jnp.concatenate(cols, axis=1)


def _shift_rows(p, prev1, prev2=None):
    row = lax.broadcasted_iota(jnp.int32, p.shape, 0)
    p1 = jnp.where(row == 0, prev1, pltpu.roll(p, 1, 0))
    if prev2 is None:
        return p1
    p2 = jnp.where(row == 0, prev2, jnp.where(row == 1, prev1, pltpu.roll(p, 2, 0)))
    return p1, p2


def _causal_conv3(p, carry, w):
    p1, p2 = _shift_rows(p, carry[7:8, :], carry[6:7, :])
    return w[0:1, :] * p2 + w[1:2, :] * p1 + w[2:3, :] * p


def _norm_cast_kernel(x_ref, g_ref, o_ref):
    o_ref[...] = _rms(x_ref[...], g_ref[...]).astype(o_ref.dtype)


def norm_cast(x, g, tm=512):
    m, d = x.shape
    return pl.pallas_call(
        _norm_cast_kernel,
        grid=(m // tm,),
        in_specs=[pl.BlockSpec((tm, d), lambda i: (i, 0)), pl.BlockSpec((1, d), lambda i: (0, 0))],
        out_specs=pl.BlockSpec((tm, d), lambda i: (i, 0)),
        out_shape=jax.ShapeDtypeStruct((m, d), BF16),
        compiler_params=_params(1),
        name="norm_cast",
    )(x, g.reshape(1, d))


def _sc_in_kernel(h_ref, wb_ref, wc_ref, wh_ref, cw_ref, o_ref, carry_ref, *, tiles_per_seq):
    @pl.when(pl.program_id(1) % tiles_per_seq == 0)
    def _():
        carry_ref[...] = jnp.zeros_like(carry_ref)

    h = h_ref[...]
    p = _dot(h, wc_ref[...]) * _dot(h, wh_ref[...])
    y = _causal_conv3(p, carry_ref[...], cw_ref[...])
    carry_ref[...] = p[p.shape[0] - 8:, :]
    o_ref[...] = (_dot(h, wb_ref[...]) * y).astype(o_ref.dtype)


def shortconv_in(h, w_in, conv_w, seq, tm=512, tn=512):
    m, d = h.shape
    nj = d // tn
    return pl.pallas_call(
        functools.partial(_sc_in_kernel, tiles_per_seq=seq // tm),
        grid=(nj, m // tm),
        in_specs=[
            pl.BlockSpec((tm, d), lambda j, i: (i, 0)),
            pl.BlockSpec((d, tn), lambda j, i: (0, j)),
            pl.BlockSpec((d, tn), lambda j, i: (0, nj + j)),
            pl.BlockSpec((d, tn), lambda j, i: (0, 2 * nj + j)),
            pl.BlockSpec((3, tn), lambda j, i: (0, j)),
        ],
        out_specs=pl.BlockSpec((tm, tn), lambda j, i: (i, j)),
        out_shape=jax.ShapeDtypeStruct((m, d), BF16),
        scratch_shapes=[pltpu.VMEM((8, tn), F32)],
        compiler_params=_params(2),
        name="shortconv_in",
    )(h, w_in, w_in, w_in, conv_w)


def _mm_norm_res_kernel(a_ref, w_ref, res_ref, g1_ref, *rest, nk, has_next):
    if has_next:
        g2_ref, x_out, h_out, acc_ref = rest
    else:
        x_out, acc_ref = rest
    k = pl.program_id(1)

    @pl.when(k == 0)
    def _():
        acc_ref[...] = jnp.zeros_like(acc_ref)

    acc_ref[...] += _dot(a_ref[...], w_ref[...])

    @pl.when(k == nk - 1)
    def _():
        xn = res_ref[...] + _rms(acc_ref[...], g1_ref[...])
        x_out[...] = xn
        if has_next:
            h_out[...] = _rms(xn, g2_ref[...]).astype(h_out.dtype)


def mm_norm_res(a, w, res, g1, g2=None, tm=512, tk=512):
    m, kdim = a.shape
    d = w.shape[1]
    nk = kdim // tk
    has_next = g2 is not None
    row = pl.BlockSpec((tm, d), lambda i, k: (i, 0))
    vec = pl.BlockSpec((1, d), lambda i, k: (0, 0))
    in_specs = [pl.BlockSpec((tm, tk), lambda i, k: (i, k)), pl.BlockSpec((tk, d), lambda i, k: (k, 0)), row, vec]
    args = [a, w, res, g1.reshape(1, d)]
    out_specs, out_shape = [row], [jax.ShapeDtypeStruct((m, d), F32)]
    if has_next:
        in_specs.append(vec)
        args.append(g2.reshape(1, d))
        out_specs.append(row)
        out_shape.append(jax.ShapeDtypeStruct((m, d), BF16))
    out = pl.pallas_call(
        functools.partial(_mm_norm_res_kernel, nk=nk, has_next=has_next),
        grid=(m // tm, nk),
        in_specs=in_specs,
        out_specs=out_specs,
        out_shape=out_shape,
        scratch_shapes=[pltpu.VMEM((tm, d), F32)],
        compiler_params=_params(2),
        name="mm_norm_res",
    )(*args)
    return tuple(out) if has_next else (out[0], None)


def _ffn_up_kernel(h_ref, wg_ref, wu_ref, cg_ref, cu_ref, bg_ref, bu_ref, o_ref, carry_g, carry_u, *, tiles_per_seq):
    @pl.when(pl.program_id(1) % tiles_per_seq == 0)
    def _():
        carry_g[...] = jnp.zeros_like(carry_g)
        carry_u[...] = jnp.zeros_like(carry_u)

    h = h_ref[...]
    pg = _dot(h, wg_ref[...])
    pu = _dot(h, wu_ref[...])
    gate = _causal_conv3(pg, carry_g[...], cg_ref[...]) + bg_ref[...]
    up = _causal_conv3(pu, carry_u[...], cu_ref[...]) + bu_ref[...]
    carry_g[...] = pg[pg.shape[0] - 8:, :]
    carry_u[...] = pu[pu.shape[0] - 8:, :]
    o_ref[...] = (gate * _sigmoid(gate) * up).astype(o_ref.dtype)


def ffn_up(h, w_up, conv_w, conv_b, seq, tm=512, tn=512):
    m, d = h.shape
    f = w_up.shape[1] // 2
    nj = f // tn
    conv_b = conv_b.reshape(1, 2 * f)
    return pl.pallas_call(
        functools.partial(_ffn_up_kernel, tiles_per_seq=seq // tm),
        grid=(nj, m // tm),
        in_specs=[
            pl.BlockSpec((tm, d), lambda j, i: (i, 0)),
            pl.BlockSpec((d, tn), lambda j, i: (0, j)),
            pl.BlockSpec((d, tn), lambda j, i: (0, nj + j)),
            pl.BlockSpec((3, tn), lambda j, i: (0, j)),
            pl.BlockSpec((3, tn), lambda j, i: (0, nj + j)),
            pl.BlockSpec((1, tn), lambda j, i: (0, j)),
            pl.BlockSpec((1, tn), lambda j, i: (0, nj + j)),
        ],
        out_specs=pl.BlockSpec((tm, tn), lambda j, i: (i, j)),
        out_shape=jax.ShapeDtypeStruct((m, f), BF16),
        scratch_shapes=[pltpu.VMEM((8, tn), F32), pltpu.VMEM((8, tn), F32)],
        compiler_params=_params(2),
        name="ffn_up",
    )(h, w_up, w_up, conv_w, conv_w, conv_b, conv_b)


def _token_shift_delta(h_ref, halo_ref, tiles_per_seq, axis):
    h = h_ref[...].astype(F32)
    prev = halo_ref[BF16_SUBLANES - 1:BF16_SUBLANES, :].astype(F32)
    prev = jnp.where(pl.program_id(axis) % tiles_per_seq == 0, 0.0, prev)
    return h, _shift_rows(h, prev) - h


def _rw_lora_kernel(h_ref, halo_ref, mu_ref, w1_ref, a1_ref, g1_ref, tw_ref, ta_ref, tg_ref, *, tiles_per_seq):
    h, xx = _token_shift_delta(h_ref, halo_ref, tiles_per_seq, 0)
    xw = (h + xx * mu_ref[1:2, :]).astype(BF16)
    xa = (h + xx * mu_ref[4:5, :]).astype(BF16)
    xg = (h + xx * mu_ref[5:6, :]).astype(BF16)
    tw_ref[...] = jnp.tanh(_dot(xw, w1_ref[...])).astype(tw_ref.dtype)
    ta_ref[...] = _dot(xa, a1_ref[...]).astype(ta_ref.dtype)
    tg_ref[...] = _sigmoid(_dot(xg, g1_ref[...])).astype(tg_ref.dtype)


def _halo_spec(tm, d, grid_axis):
    step = tm // BF16_SUBLANES
    if grid_axis == 0:
        return pl.BlockSpec((BF16_SUBLANES, d), lambda i: (jnp.maximum(i * step - 1, 0), 0))
    return pl.BlockSpec((BF16_SUBLANES, d), lambda j, i: (jnp.maximum(i * step - 1, 0), 0))


def rwkv_lora(h, mu, w1, a1, g1, seq, tm=512):
    m, d = h.shape
    full = lambda arr: pl.BlockSpec(arr.shape, lambda i: (0, 0))
    outs = [w1.shape[1], a1.shape[1], g1.shape[1]]
    return pl.pallas_call(
        functools.partial(_rw_lora_kernel, tiles_per_seq=seq // tm),
        grid=(m // tm,),
        in_specs=[pl.BlockSpec((tm, d), lambda i: (i, 0)), _halo_spec(tm, d, 0), full(mu), full(w1), full(a1), full(g1)],
        out_specs=[pl.BlockSpec((tm, n), lambda i: (i, 0)) for n in outs],
        out_shape=[jax.ShapeDtypeStruct((m, n), BF16) for n in outs],
        compiler_params=_params(1),
        name="rwkv_lora",
    )(h, h, mu, w1, a1, g1)


def _rw_proj_kernel(h_ref, halo_ref, mu_ref, wr_ref, wk_ref, wv_ref, tw_ref, ta_ref, tg_ref,
                    w2_ref, a2_ref, g2_ref, w0_ref, a0_ref, kk_ref, ka_ref, bd_ref,
                    r_out, ld_out, k_out, v_out, av_out, bv_out, g_out, *, tiles_per_seq):
    h, xx = _token_shift_delta(h_ref, halo_ref, tiles_per_seq, 1)
    xr = (h + xx * mu_ref[0:1, :]).astype(BF16)
    xk = (h + xx * mu_ref[2:3, :]).astype(BF16)
    xv = (h + xx * mu_ref[3:4, :]).astype(BF16)
    r = _dot(xr, wr_ref[...])
    k = _dot(xk, wk_ref[...])
    v = _dot(xv, wv_ref[...])
    wl = w0_ref[...] + _dot(tw_ref[...], w2_ref[...])
    z = -wl
    w_log = -(jnp.maximum(z, 0.0) + jnp.log(1.0 + jnp.exp(-jnp.abs(z)))) - 0.5
    ld = -jnp.exp(w_log)
    a = _sigmoid(a0_ref[...] + _dot(ta_ref[...], a2_ref[...]))
    g = _dot(tg_ref[...], g2_ref[...])
    kk = k * kk_ref[...]
    nrm = jnp.sqrt(_head_sum(kk * kk, bd_ref[...]))
    kk = kk / jnp.maximum(nrm, L2_EPS)
    k = k * (1.0 + (a - 1.0) * ka_ref[...])
    r_out[...] = r.astype(r_out.dtype)
    ld_out[...] = ld
    k_out[...] = k.astype(k_out.dtype)
    v_out[...] = v.astype(v_out.dtype)
    av_out[...] = (-kk).astype(av_out.dtype)
    bv_out[...] = (kk * a).astype(bv_out.dtype)
    g_out[...] = g.astype(g_out.dtype)


def rwkv_proj(h, mu, wr, wk, wv, tw, ta, tg, w2, a2, g2, w0, a0, k_k, k_a, bd, seq, tm=512, tn=512):
    m, d = h.shape
    colw = lambda arr: pl.BlockSpec((arr.shape[0], tn), lambda j, i: (0, j))
    rowt = lambda arr: pl.BlockSpec((tm, arr.shape[1]), lambda j, i: (i, 0))
    vecs = [w0.reshape(1, d), a0.reshape(1, d), k_k.reshape(1, d), k_a.reshape(1, d)]
    out_dtypes = [BF16, F32, BF16, BF16, BF16, BF16, BF16]
    return pl.pallas_call(
        functools.partial(_rw_proj_kernel, tiles_per_seq=seq // tm),
        grid=(d // tn, m // tm),
        in_specs=[pl.BlockSpec((tm, d), lambda j, i: (i, 0)), _halo_spec(tm, d, 1),
                  pl.BlockSpec(mu.shape, lambda j, i: (0, 0)),
                  colw(wr), colw(wk), colw(wv), rowt(tw), rowt(ta), rowt(tg), colw(w2), colw(a2), colw(g2)]
                 + [colw(x) for x in vecs] + [pl.BlockSpec(bd.shape, lambda j, i: (0, 0))],
        out_specs=[pl.BlockSpec((tm, tn), lambda j, i: (i, j)) for _ in out_dtypes],
        out_shape=[jax.ShapeDtypeStruct((m, d), dt) for dt in out_dtypes],
        compiler_params=_params(2),
        name="rwkv_proj",
    )(h, h, mu, wr, wk, wv, tw, ta, tg, w2, a2, g2, *vecs, bd)


def _stack_pair(x, blk):
    return jnp.where(blk, jnp.concatenate([x, x], axis=0), 0.0).astype(BF16)


def _rw_scan_kernel(r_ref, ld_ref, k_ref, v_ref, av_ref, bv_ref, g_ref, rk_ref, gng_ref, gnb_ref, bd_ref,
                    o_ref, s_ref, *, n_pairs):
    L = CHUNK

    @pl.when(pl.program_id(2) == 0)
    def _():
        s_ref[...] = jnp.zeros_like(s_ref)

    n2 = 2 * L
    row = lax.broadcasted_iota(jnp.int32, (n2, n2), 0)
    col = lax.broadcasted_iota(jnp.int32, (n2, n2), 1)
    same = (row // L) == (col // L)
    strict = same & (row > col)
    incl = same & (row >= col)
    eye = (row == col).astype(F32)
    tr = lax.broadcasted_iota(jnp.int32, (L, L), 0)
    tc = lax.broadcasted_iota(jnp.int32, (L, L), 1)
    tri = (tr >= tc).astype(BF16)
    bd = bd_ref[...]

    for p in range(n_pairs):
        sl = slice(p * PAIR, (p + 1) * PAIR)
        ld = ld_ref[:, sl]
        hi, mid, lo = _split3(ld)
        cum = _dot(tri, hi) + _dot(tri, mid) + _dot(tri, lo)
        e_pos = jnp.exp(cum)
        e_neg = jnp.exp(-cum)
        r = r_ref[:, sl].astype(F32)
        k = k_ref[:, sl].astype(F32)
        v = v_ref[:, sl].astype(F32)
        rt = _stack_pair(r * e_pos, same)
        kt = _stack_pair(k * e_neg, same)
        bt = _stack_pair(bv_ref[:, sl].astype(F32) * e_neg, same)
        at = _stack_pair(av_ref[:, sl].astype(F32) * jnp.exp(cum - ld), same)
        vs = _stack_pair(v, same)

        a_ab = jnp.where(strict, _dot_nt(at, bt), 0.0)
        a_ak = jnp.where(strict, _dot_nt(at, kt), 0.0).astype(BF16)
        a_rb = jnp.where(incl, _dot_nt(rt, bt), 0.0).astype(BF16)
        a_rk = jnp.where(incl, _dot_nt(rt, kt), 0.0).astype(BF16)

        inv = eye + a_ab
        pw = a_ab
        for _ in range(L.bit_length() - 2):
            pwb = pw.astype(BF16)
            pw = _dot(pwb, pwb)
            inv = inv + _dot(inv.astype(BF16), pw.astype(BF16))

        s = s_ref[p]
        sb = s.astype(BF16)
        u = _dot(inv.astype(BF16), (_dot_nt(at, sb) + _dot(a_ak, vs)).astype(BF16))
        ub = u.astype(BF16)
        y2 = _dot_nt(rt, sb) + _dot(a_rb, ub) + _dot(a_rk, vs)
        w_last = e_pos[L - 1:L, :]
        s_ref[p] = (s + _dot_tn(ub, bt) + _dot_tn(vs, kt)) * w_last
        y = y2[:L, :] + y2[L:, :]

        mean = _head_sum(y, bd) * (1.0 / HEAD_SIZE)
        yc = y - mean
        var = _head_sum(yc * yc, bd) * (1.0 / HEAD_SIZE)
        yn = yc * lax.rsqrt(var + GN_EPS) * gng_ref[:, sl] + gnb_ref[:, sl]
        bonus = _head_sum(r * k * rk_ref[:, sl], bd) * v
        o_ref[:, sl] = ((yn + bonus) * g_ref[:, sl].astype(F32)).astype(o_ref.dtype)


def rwkv_scan(r, ld, k, v, av, bv, g, r_k, gn_g, gn_b, bd, seq, n_pairs=4):
    m, d = r.shape
    cw = n_pairs * PAIR
    nc = seq // CHUNK
    blk = pl.BlockSpec((CHUNK, cw), lambda b, j, c: (b * nc + c, j))
    vec = pl.BlockSpec((1, cw), lambda b, j, c: (0, j))
    return pl.pallas_call(
        functools.partial(_rw_scan_kernel, n_pairs=n_pairs),
        grid=(m // seq, d // cw, nc),
        in_specs=[blk] * 7 + [vec] * 3 + [pl.BlockSpec(bd.shape, lambda b, j, c: (0, 0))],
        out_specs=blk,
        out_shape=jax.ShapeDtypeStruct((m, d), BF16),
        scratch_shapes=[pltpu.VMEM((n_pairs, PAIR, PAIR), F32)],
        compiler_params=_params(3),
        name="rwkv_scan",
    )(r, ld, k, v, av, bv, g, r_k.reshape(1, d), gn_g.reshape(1, d), gn_b.reshape(1, d), bd)


def _pad_cols(w, n):
    return jnp.pad(w, ((0, 0), (0, n - w.shape[1])))


def _pad_rows(w, n):
    return jnp.pad(w, ((0, n - w.shape[0]), (0, 0)))


def kernel(x, norm_g, sc_w_in, sc_conv, sc_w_out, rw_mu, rw_wr, rw_wk, rw_wv, rw_wo, rw_w0, rw_w1, rw_w2,
           rw_a0, rw_a1, rw_a2, rw_g1, rw_g2, rw_kk, rw_ka, rw_rk, rw_gn_g, rw_gn_b,
           ffn_w_up, ffn_conv, ffn_conv_b, ffn_w_down):
    bsz, seq, d = x.shape
    depth = norm_g.shape[0]
    xf = x.reshape(bsz * seq, d)
    lane = jnp.arange(LANES)
    bd = (lane[:, None] // HEAD_SIZE == lane[None, :] // HEAD_SIZE).astype(BF16)
    bf = lambda w: w.astype(BF16)

    h = norm_cast(xf, norm_g[0, 0])
    for i in range(depth):
        j = i // 2
        if i % 2 == 0:
            mix = shortconv_in(h, bf(sc_w_in[j]), sc_conv[j], seq)
            w_mix_out = bf(sc_w_out[j])
        else:
            lora = LANES * pl.cdiv(rw_w1.shape[2], LANES)
            tw, ta, tg = rwkv_lora(h, rw_mu[j], bf(_pad_cols(rw_w1[j], lora)), bf(_pad_cols(rw_a1[j], lora)),
                                   bf(rw_g1[j]), seq)
            parts = rwkv_proj(h, rw_mu[j], bf(rw_wr[j]), bf(rw_wk[j]), bf(rw_wv[j]), tw, ta, tg,
                              bf(_pad_rows(rw_w2[j], lora)), bf(_pad_rows(rw_a2[j], lora)), bf(rw_g2[j]),
                              rw_w0[j], rw_a0[j], rw_kk[j], rw_ka[j], bd, seq)
            mix = rwkv_scan(*parts, rw_rk[j], rw_gn_g[j], rw_gn_b[j], bd, seq)
            w_mix_out = bf(rw_wo[j])
        xf, h = mm_norm_res(mix, w_mix_out, xf, norm_g[i, 1], norm_g[i, 2], tk=d // 2)
        act = ffn_up(h, bf(ffn_w_up[i]), ffn_conv[i], ffn_conv_b[i], seq)
        g_next = norm_g[i + 1, 0] if i + 1 < depth else None
        xf, h = mm_norm_res(act, bf(ffn_w_down[i]), xf, norm_g[i, 3], g_next)
    return xf.reshape(bsz, seq, d)
```

```python
import functools

import jax
import jax.numpy as jnp
from jax import lax
from jax.experimental import pallas as pl
from jax.experimental.pallas import tpu as pltpu

F32 = jnp.float32
BF16 = jnp.bfloat16

HEAD_SIZE = 64
RMS_EPS = 1e-6
GN_EPS = 64e-5
L2_EPS = 1e-12

LANES = 128
BF16_SUBLANES = 16
VMEM_LIMIT_BYTES = 56 * 1024 * 1024
CHUNK = 64
PAIR = 2 * HEAD_SIZE


def _params(n_axes):
    return pltpu.CompilerParams(dimension_semantics=("arbitrary",) * n_axes,
                                vmem_limit_bytes=VMEM_LIMIT_BYTES)


def _dot(a, b):
    return jnp.dot(a, b, preferred_element_type=F32)


def _dot_nt(a, b):
    return lax.dot_general(a, b, (((1,), (1,)), ((), ())), preferred_element_type=F32)


def _dot_tn(a, b):
    return lax.dot_general(a, b, (((0,), (0,)), ((), ())), preferred_element_type=F32)


def _split2(x):
    hi = x.astype(BF16)
    lo = (x - hi.astype(F32)).astype(BF16)
    return hi, lo


def _split3(x):
    hi = x.astype(BF16)
    r1 = x - hi.astype(F32)
    mid = r1.astype(BF16)
    lo = (r1 - mid.astype(F32)).astype(BF16)
    return hi, mid, lo


def _rms(y, g):
    return y * lax.rsqrt(jnp.mean(y * y, axis=-1, keepdims=True) + RMS_EPS) * g


def _sigmoid(x):
    return 1.0 / (1.0 + jnp.exp(-x))


def _head_sum(x, bd):
    rows, n = x.shape[0], x.shape[1] // LANES
    if n > 1:
        x = jnp.concatenate([x[:, s * LANES:(s + 1) * LANES] for s in range(n)], axis=0)
    hi, lo = _split2(x)
    out = _dot(hi, bd) + _dot(lo, bd)
    if n > 1:
        out = jnp.concatenate([out[s * rows:(s + 1) * rows, :] for s in range(n)], axis=1)
    return out


def _shift_rows(p, prev1, prev2=None):
    row = lax.broadcasted_iota(jnp.int32, p.shape, 0)
    p1 = jnp.where(row == 0, prev1, pltpu.roll(p, 1, 0))
    if prev2 is None:
        return p1
    p2 = jnp.where(row == 0, prev2, jnp.where(row == 1, prev1, pltpu.roll(p, 2, 0)))
    return p1, p2


def _causal_conv3(p, carry, w):
    p1, p2 = _shift_rows(p, carry[7:8, :], carry[6:7, :])
    return w[0:1, :] * p2 + w[1:2, :] * p1 + w[2:3, :] * p


def _norm_cast_kernel(x_ref, g_ref, o_ref):
    o_ref[...] = _rms(x_ref[...], g_ref[...]).astype(o_ref.dtype)


def norm_cast(x, g, tm=512):
    m, d = x.shape
    return pl.pallas_call(
        _norm_cast_kernel,
        grid=(m // tm,),
        in_specs=[pl.BlockSpec((tm, d), lambda i: (i, 0)), pl.BlockSpec((1, d), lambda i: (0, 0))],
        out_specs=pl.BlockSpec((tm, d), lambda i: (i, 0)),
        out_shape=jax.ShapeDtypeStruct((m, d), BF16),
        compiler_params=_params(1),
        name="norm_cast",
    )(x, g.reshape(1, d))


def _sc_in_kernel(h_ref, wb_ref, wc_ref, wh_ref, cw_ref, o_ref, carry_ref, *, tiles_per_seq):
    @pl.when(pl.program_id(1) % tiles_per_seq == 0)
    def _():
        carry_ref[...] = jnp.zeros_like(carry_ref)

    h = h_ref[...]
    p = _dot(h, wc_ref[...]) * _dot(h, wh_ref[...])
    y = _causal_conv3(p, carry_ref[...], cw_ref[...])
    carry_ref[...] = p[p.shape[0] - 8:, :]
    o_ref[...] = (_dot(h, wb_ref[...]) * y).astype(o_ref.dtype)


def shortconv_in(h, w_in, conv_w, seq, tm=512, tn=512):
    m, d = h.shape
    nj = d // tn
    return pl.pallas_call(
        functools.partial(_sc_in_kernel, tiles_per_seq=seq // tm),
        grid=(nj, m // tm),
        in_specs=[
            pl.BlockSpec((tm, d), lambda j, i: (i, 0)),
            pl.BlockSpec((d, tn), lambda j, i: (0, j)),
            pl.BlockSpec((d, tn), lambda j, i: (0, nj + j)),
            pl.BlockSpec((d, tn), lambda j, i: (0, 2 * nj + j)),
            pl.BlockSpec((3, tn), lambda j, i: (0, j)),
        ],
        out_specs=pl.BlockSpec((tm, tn), lambda j, i: (i, j)),
        out_shape=jax.ShapeDtypeStruct((m, d), BF16),
        scratch_shapes=[pltpu.VMEM((8, tn), F32)],
        compiler_params=_params(2),
        name="shortconv_in",
    )(h, w_in, w_in, w_in, conv_w)


def _mm_norm_res_kernel(a_ref, w_ref, res_ref, g1_ref, *rest, nk, has_next):
    if has_next:
        g2_ref, x_out, h_out, acc_ref = rest
    else:
        x_out, acc_ref = rest
    k = pl.program_id(1)

    @pl.when(k == 0)
    def _():
        acc_ref[...] = jnp.zeros_like(acc_ref)

    acc_ref[...] += _dot(a_ref[...], w_ref[...])

    @pl.when(k == nk - 1)
    def _():
        xn = res_ref[...] + _rms(acc_ref[...], g1_ref[...])
        x_out[...] = xn
        if has_next:
            h_out[...] = _rms(xn, g2_ref[...]).astype(h_out.dtype)


def mm_norm_res(a, w, res, g1, g2=None, tm=512, tk=512):
    m, kdim = a.shape
    d = w.shape[1]
    nk = kdim // tk
    has_next = g2 is not None
    row = pl.BlockSpec((tm, d), lambda i, k: (i, 0))
    vec = pl.BlockSpec((1, d), lambda i, k: (0, 0))
    in_specs = [pl.BlockSpec((tm, tk), lambda i, k: (i, k)), pl.BlockSpec((tk, d), lambda i, k: (k, 0)), row, vec]
    args = [a, w, res, g1.reshape(1, d)]
    out_specs, out_shape = [row], [jax.ShapeDtypeStruct((m, d), F32)]
    if has_next:
        in_specs.append(vec)
        args.append(g2.reshape(1, d))
        out_specs.append(row)
        out_shape.append(jax.ShapeDtypeStruct((m, d), BF16))
    out = pl.pallas_call(
        functools.partial(_mm_norm_res_kernel, nk=nk, has_next=has_next),
        grid=(m // tm, nk),
        in_specs=in_specs,
        out_specs=out_specs,
        out_shape=out_shape,
        scratch_shapes=[pltpu.VMEM((tm, d), F32)],
        compiler_params=_params(2),
        name="mm_norm_res",
    )(*args)
    return tuple(out) if has_next else (out[0], None)


def _ffn_up_kernel(h_ref, wg_ref, wu_ref, cg_ref, cu_ref, bg_ref, bu_ref, o_ref, carry_g, carry_u, *, tiles_per_seq):
    @pl.when(pl.program_id(1) % tiles_per_seq == 0)
    def _():
        carry_g[...] = jnp.zeros_like(carry_g)
        carry_u[...] = jnp.zeros_like(carry_u)

    h = h_ref[...]
    pg = _dot(h, wg_ref[...])
    pu = _dot(h, wu_ref[...])
    gate = _causal_conv3(pg, carry_g[...], cg_ref[...]) + bg_ref[...]
    up = _causal_conv3(pu, carry_u[...], cu_ref[...]) + bu_ref[...]
    carry_g[...] = pg[pg.shape[0] - 8:, :]
    carry_u[...] = pu[pu.shape[0] - 8:, :]
    o_ref[...] = (gate * _sigmoid(gate) * up).astype(o_ref.dtype)


def ffn_up(h, w_up, conv_w, conv_b, seq, tm=512, tn=512):
    m, d = h.shape
    f = w_up.shape[1] // 2
    nj = f // tn
    conv_b = conv_b.reshape(1, 2 * f)
    return pl.pallas_call(
        functools.partial(_ffn_up_kernel, tiles_per_seq=seq // tm),
        grid=(nj, m // tm),
        in_specs=[
            pl.BlockSpec((tm, d), lambda j, i: (i, 0)),
            pl.BlockSpec((d, tn), lambda j, i: (0, j)),
            pl.BlockSpec((d, tn), lambda j, i: (0, nj + j)),
            pl.BlockSpec((3, tn), lambda j, i: (0, j)),
            pl.BlockSpec((3, tn), lambda j, i: (0, nj + j)),
            pl.BlockSpec((1, tn), lambda j, i: (0, j)),
            pl.BlockSpec((1, tn), lambda j, i: (0, nj + j)),
        ],
        out_specs=pl.BlockSpec((tm, tn), lambda j, i: (i, j)),
        out_shape=jax.ShapeDtypeStruct((m, f), BF16),
        scratch_shapes=[pltpu.VMEM((8, tn), F32), pltpu.VMEM((8, tn), F32)],
        compiler_params=_params(2),
        name="ffn_up",
    )(h, w_up, w_up, conv_w, conv_w, conv_b, conv_b)


def _token_shift_delta(h_ref, halo_ref, tiles_per_seq, axis):
    h = h_ref[...].astype(F32)
    prev = halo_ref[BF16_SUBLANES - 1:BF16_SUBLANES, :].astype(F32)
    prev = jnp.where(pl.program_id(axis) % tiles_per_seq == 0, 0.0, prev)
    return h, _shift_rows(h, prev) - h


def _rw_lora_kernel(h_ref, halo_ref, mu_ref, w1_ref, a1_ref, g1_ref, tw_ref, ta_ref, tg_ref, *, tiles_per_seq):
    h, xx = _token_shift_delta(h_ref, halo_ref, tiles_per_seq, 0)
    xw = (h + xx * mu_ref[1:2, :]).astype(BF16)
    xa = (h + xx * mu_ref[4:5, :]).astype(BF16)
    xg = (h + xx * mu_ref[5:6, :]).astype(BF16)
    tw_ref[...] = jnp.tanh(_dot(xw, w1_ref[...])).astype(tw_ref.dtype)
    ta_ref[...] = _dot(xa, a1_ref[...]).astype(ta_ref.dtype)
    tg_ref[...] = _sigmoid(_dot(xg, g1_ref[...])).astype(tg_ref.dtype)


def _halo_spec(tm, d, grid_axis):
    step = tm // BF16_SUBLANES
    if grid_axis == 0:
        return pl.BlockSpec((BF16_SUBLANES, d), lambda i: (jnp.maximum(i * step - 1, 0), 0))
    return pl.BlockSpec((BF16_SUBLANES, d), lambda j, i: (jnp.maximum(i * step - 1, 0), 0))


def rwkv_lora(h, mu, w1, a1, g1, seq, tm=512):
    m, d = h.shape
    full = lambda arr: pl.BlockSpec(arr.shape, lambda i: (0, 0))
    outs = [w1.shape[1], a1.shape[1], g1.shape[1]]
    return pl.pallas_call(
        functools.partial(_rw_lora_kernel, tiles_per_seq=seq // tm),
        grid=(m // tm,),
        in_specs=[pl.BlockSpec((tm, d), lambda i: (i, 0)), _halo_spec(tm, d, 0), full(mu), full(w1), full(a1), full(g1)],
        out_specs=[pl.BlockSpec((tm, n), lambda i: (i, 0)) for n in outs],
        out_shape=[jax.ShapeDtypeStruct((m, n), BF16) for n in outs],
        compiler_params=_params(1),
        name="rwkv_lora",
    )(h, h, mu, w1, a1, g1)


def _rw_proj_kernel(h_ref, halo_ref, mu_ref, wr_ref, wk_ref, wv_ref, tw_ref, ta_ref, tg_ref,
                    w2_ref, a2_ref, g2_ref, w0_ref, a0_ref, kk_ref, ka_ref, bd_ref,
                    r_out, ld_out, k_out, v_out, av_out, bv_out, g_out, *, tiles_per_seq):
    h, xx = _token_shift_delta(h_ref, halo_ref, tiles_per_seq, 1)
    xr = (h + xx * mu_ref[0:1, :]).astype(BF16)
    xk = (h + xx * mu_ref[2:3, :]).astype(BF16)
    xv = (h + xx * mu_ref[3:4, :]).astype(BF16)
    r = _dot(xr, wr_ref[...])
    k = _dot(xk, wk_ref[...])
    v = _dot(xv, wv_ref[...])
    wl = w0_ref[...] + _dot(tw_ref[...], w2_ref[...])
    z = -wl
    w_log = -(jnp.maximum(z, 0.0) + jnp.log(1.0 + jnp.exp(-jnp.abs(z)))) - 0.5
    ld = -jnp.exp(w_log)
    a = _sigmoid(a0_ref[...] + _dot(ta_ref[...], a2_ref[...]))
    g = _dot(tg_ref[...], g2_ref[...])
    kk = k * kk_ref[...]
    nrm = jnp.sqrt(_head_sum(kk * kk, bd_ref[...]))
    kk = kk / jnp.maximum(nrm, L2_EPS)
    k = k * (1.0 + (a - 1.0) * ka_ref[...])
    r_out[...] = r.astype(r_out.dtype)
    ld_out[...] = ld
    k_out[...] = k.astype(k_out.dtype)
    v_out[...] = v.astype(v_out.dtype)
    av_out[...] = (-kk).astype(av_out.dtype)
    bv_out[...] = (kk * a).astype(bv_out.dtype)
    g_out[...] = g.astype(g_out.dtype)


def rwkv_proj(h, mu, wr, wk, wv, tw, ta, tg, w2, a2, g2, w0, a0, k_k, k_a, bd, seq, tm=512, tn=512):
    m, d = h.shape
    colw = lambda arr: pl.BlockSpec((arr.shape[0], tn), lambda j, i: (0, j))
    rowt = lambda arr: pl.BlockSpec((tm, arr.shape[1]), lambda j, i: (i, 0))
    vecs = [w0.reshape(1, d), a0.reshape(1, d), k_k.reshape(1, d), k_a.reshape(1, d)]
    out_dtypes = [BF16, F32, BF16, BF16, BF16, BF16, BF16]
    return pl.pallas_call(
        functools.partial(_rw_proj_kernel, tiles_per_seq=seq // tm),
        grid=(d // tn, m // tm),
        in_specs=[pl.BlockSpec((tm, d), lambda j, i: (i, 0)), _halo_spec(tm, d, 1),
                  pl.BlockSpec(mu.shape, lambda j, i: (0, 0)),
                  colw(wr), colw(wk), colw(wv), rowt(tw), rowt(ta), rowt(tg), colw(w2), colw(a2), colw(g2)]
                 + [colw(x) for x in vecs] + [pl.BlockSpec(bd.shape, lambda j, i: (0, 0))],
        out_specs=[pl.BlockSpec((tm, tn), lambda j, i: (i, j)) for _ in out_dtypes],
        out_shape=[jax.ShapeDtypeStruct((m, d), dt) for dt in out_dtypes],
        compiler_params=_params(2),
        name="rwkv_proj",
    )(h, h, mu, wr, wk, wv, tw, ta, tg, w2, a2, g2, *vecs, bd)


def _stack_pair(x, blk):
    return jnp.where(blk, jnp.concatenate([x, x], axis=0), 0.0).astype(BF16)


def _rw_scan_kernel(r_ref, ld_ref, k_ref, v_ref, av_ref, bv_ref, g_ref, rk_ref, gng_ref, gnb_ref, bd_ref,
                    o_ref, s_ref, *, n_pairs):
    L = CHUNK

    @pl.when(pl.program_id(2) == 0)
    def _():
        s_ref[...] = jnp.zeros_like(s_ref)

    n2 = 2 * L
    row = lax.broadcasted_iota(jnp.int32, (n2, n2), 0)
    col = lax.broadcasted_iota(jnp.int32, (n2, n2), 1)
    same = (row // L) == (col // L)
    strict = same & (row > col)
    incl = same & (row >= col)
    eye = (row == col).astype(F32)
    tr = lax.broadcasted_iota(jnp.int32, (L, L), 0)
    tc = lax.broadcasted_iota(jnp.int32, (L, L), 1)
    tri = (tr >= tc).astype(BF16)
    bd = bd_ref[...]

    pairs = range(n_pairs)
    cols = [slice(p * PAIR, (p + 1) * PAIR) for p in pairs]
    ld = ld_ref[...]
    hi, mid, lo = _split3(ld)
    cum = _dot(tri, hi) + _dot(tri, mid) + _dot(tri, lo)
    e_pos = jnp.exp(cum)
    e_neg = jnp.exp(-cum)
    r = r_ref[...].astype(F32)
    k = k_ref[...].astype(F32)
    v = v_ref[...].astype(F32)
    rt_w = r * e_pos
    kt_w = k * e_neg
    bt_w = bv_ref[...].astype(F32) * e_neg
    at_w = av_ref[...].astype(F32) * jnp.exp(cum - ld)
    rt = [_stack_pair(rt_w[:, c], same) for c in cols]
    kt = [_stack_pair(kt_w[:, c], same) for c in cols]
    bt = [_stack_pair(bt_w[:, c], same) for c in cols]
    at = [_stack_pair(at_w[:, c], same) for c in cols]
    vs = [_stack_pair(v[:, c], same) for c in cols]

    a_ab = [jnp.where(strict, _dot_nt(at[p], bt[p]), 0.0) for p in pairs]
    a_ak = [jnp.where(strict, _dot_nt(at[p], kt[p]), 0.0).astype(BF16) for p in pairs]
    a_rb = [jnp.where(incl, _dot_nt(rt[p], bt[p]), 0.0).astype(BF16) for p in pairs]
    a_rk = [jnp.where(incl, _dot_nt(rt[p], kt[p]), 0.0).astype(BF16) for p in pairs]

    inv = [eye + a for a in a_ab]
    pw = a_ab
    for _ in range(L.bit_length() - 2):
        pwb = [x.astype(BF16) for x in pw]
        pw = [_dot(x, x) for x in pwb]
        inv = [i + _dot(i.astype(BF16), q.astype(BF16)) for i, q in zip(inv, pw)]
    invb = [i.astype(BF16) for i in inv]

    s = [s_ref[p] for p in pairs]
    sb = [x.astype(BF16) for x in s]
    rhs = [(_dot_nt(at[p], sb[p]) + _dot(a_ak[p], vs[p])).astype(BF16) for p in pairs]
    ub = [_dot(invb[p], rhs[p]).astype(BF16) for p in pairs]
    y2 = [_dot_nt(rt[p], sb[p]) + _dot(a_rb[p], ub[p]) + _dot(a_rk[p], vs[p]) for p in pairs]
    w_last = e_pos[L - 1:L, :]
    for p in pairs:
        s_ref[p] = (s[p] + _dot_tn(ub[p], bt[p]) + _dot_tn(vs[p], kt[p])) * w_last[:, cols[p]]
    y = jnp.concatenate([x[:L, :] + x[L:, :] for x in y2], axis=1)

    mean = _head_sum(y, bd) * (1.0 / HEAD_SIZE)
    yc = y - mean
    var = _head_sum(yc * yc, bd) * (1.0 / HEAD_SIZE)
    yn = yc * lax.rsqrt(var + GN_EPS) * gng_ref[...] + gnb_ref[...]
    bonus = _head_sum(r * k * rk_ref[...], bd) * v
    o_ref[...] = ((yn + bonus) * g_ref[...].astype(F32)).astype(o_ref.dtype)


def rwkv_scan(r, ld, k, v, av, bv, g, r_k, gn_g, gn_b, bd, seq, n_pairs=16):
    m, d = r.shape
    n_pairs = min(n_pairs, d // PAIR)
    cw = n_pairs * PAIR
    nc = seq // CHUNK
    blk = pl.BlockSpec((CHUNK, cw), lambda b, j, c: (b * nc + c, j))
    vec = pl.BlockSpec((1, cw), lambda b, j, c: (0, j))
    return pl.pallas_call(
        functools.partial(_rw_scan_kernel, n_pairs=n_pairs),
        grid=(m // seq, d // cw, nc),
        in_specs=[blk] * 7 + [vec] * 3 + [pl.BlockSpec(bd.shape, lambda b, j, c: (0, 0))],
        out_specs=blk,
        out_shape=jax.ShapeDtypeStruct((m, d), BF16),
        scratch_shapes=[pltpu.VMEM((n_pairs, PAIR, PAIR), F32)],
        compiler_params=_params(3),
        name="rwkv_scan",
    )(r, ld, k, v, av, bv, g, r_k.reshape(1, d), gn_g.reshape(1, d), gn_b.reshape(1, d), bd)


def _pad_cols(w, n):
    return jnp.pad(w, ((0, 0), (0, n - w.shape[1])))


def _pad_rows(w, n):
    return jnp.pad(w, ((0, n - w.shape[0]), (0, 0)))


def kernel(x, norm_g, sc_w_in, sc_conv, sc_w_out, rw_mu, rw_wr, rw_wk, rw_wv, rw_wo, rw_w0, rw_w1, rw_w2,
           rw_a0, rw_a1, rw_a2, rw_g1, rw_g2, rw_kk, rw_ka, rw_rk, rw_gn_g, rw_gn_b,
           ffn_w_up, ffn_conv, ffn_conv_b, ffn_w_down):
    bsz, seq, d = x.shape
    depth = norm_g.shape[0]
    xf = x.reshape(bsz * seq, d)
    lane = jnp.arange(LANES)
    bd = (lane[:, None] // HEAD_SIZE == lane[None, :] // HEAD_SIZE).astype(BF16)
    bf = lambda w: w.astype(BF16)

    h = norm_cast(xf, norm_g[0, 0])
    for i in range(depth):
        j = i // 2
        if i % 2 == 0:
            mix = shortconv_in(h, bf(sc_w_in[j]), sc_conv[j], seq)
            w_mix_out = bf(sc_w_out[j])
        else:
            lora = LANES * pl.cdiv(rw_w1.shape[2], LANES)
            tw, ta, tg = rwkv_lora(h, rw_mu[j], bf(_pad_cols(rw_w1[j], lora)), bf(_pad_cols(rw_a1[j], lora)),
                                   bf(rw_g1[j]), seq)
            parts = rwkv_proj(h, rw_mu[j], bf(rw_wr[j]), bf(rw_wk[j]), bf(rw_wv[j]), tw, ta, tg,
                              bf(_pad_rows(rw_w2[j], lora)), bf(_pad_rows(rw_a2[j], lora)), bf(rw_g2[j]),
                              rw_w0[j], rw_a0[j], rw_kk[j], rw_ka[j], bd, seq)
            mix = rwkv_scan(*parts, rw_rk[j], rw_gn_g[j], rw_gn_b[j], bd, seq)
            w_mix_out = bf(rw_wo[j])
        xf, h = mm_norm_res(mix, w_mix_out, xf, norm_g[i, 1], norm_g[i, 2], tk=d // 2)
        act = ffn_up(h, bf(ffn_w_up[i]), ffn_conv[i], ffn_conv_b[i], seq)
        g_next = norm_g[i + 1, 0] if i + 1 < depth else None
        xf, h = mm_norm_res(act, bf(ffn_w_down[i]), xf, norm_g[i, 3], g_next)
    return xf.reshape(bsz, seq, d)
```

```python
import functools

import jax
import jax.numpy as jnp
from jax import lax
from jax.experimental import pallas as pl
from jax.experimental.pallas import tpu as pltpu

F32 = jnp.float32
BF16 = jnp.bfloat16

HEAD_SIZE = 64
RMS_EPS = 1e-6
GN_EPS = 64e-5
L2_EPS = 1e-12

LANES = 128
BF16_SUBLANES = 16
VMEM_LIMIT_BYTES = 56 * 1024 * 1024
CHUNK = 64
PAIR = 2 * HEAD_SIZE


def _params(n_axes):
    return pltpu.CompilerParams(dimension_semantics=("arbitrary",) * n_axes,
                                vmem_limit_bytes=VMEM_LIMIT_BYTES)


def _dot(a, b):
    return jnp.dot(a, b, preferred_element_type=F32)


def _dot_nt(a, b):
    return lax.dot_general(a, b, (((1,), (1,)), ((), ())), preferred_element_type=F32)


def _dot_tn(a, b):
    return lax.dot_general(a, b, (((0,), (0,)), ((), ())), preferred_element_type=F32)


def _split2(x):
    hi = x.astype(BF16)
    lo = (x - hi.astype(F32)).astype(BF16)
    return hi, lo


def _split3(x):
    hi = x.astype(BF16)
    r1 = x - hi.astype(F32)
    mid = r1.astype(BF16)
    lo = (r1 - mid.astype(F32)).astype(BF16)
    return hi, mid, lo


def _rms(y, g):
    return y * lax.rsqrt(jnp.mean(y * y, axis=-1, keepdims=True) + RMS_EPS) * g


def _sigmoid(x):
    return 1.0 / (1.0 + jnp.exp(-x))


def _head_sum(x, bd):
    rows, n = x.shape[0], x.shape[1] // LANES
    if n > 1:
        x = jnp.concatenate([x[:, s * LANES:(s + 1) * LANES] for s in range(n)], axis=0)
    hi, lo = _split2(x)
    out = _dot(hi, bd) + _dot(lo, bd)
    if n > 1:
        out = jnp.concatenate([out[s * rows:(s + 1) * rows, :] for s in range(n)], axis=1)
    return out


def _shift_rows(p, prev1, prev2=None):
    row = lax.broadcasted_iota(jnp.int32, p.shape, 0)
    p1 = jnp.where(row == 0, prev1, pltpu.roll(p, 1, 0))
    if prev2 is None:
        return p1
    p2 = jnp.where(row == 0, prev2, jnp.where(row == 1, prev1, pltpu.roll(p, 2, 0)))
    return p1, p2


def _causal_conv3(p, carry, w):
    p1, p2 = _shift_rows(p, carry[7:8, :], carry[6:7, :])
    return w[0:1, :] * p2 + w[1:2, :] * p1 + w[2:3, :] * p


def _norm_cast_kernel(x_ref, g_ref, o_ref):
    o_ref[...] = _rms(x_ref[...], g_ref[...]).astype(o_ref.dtype)


def norm_cast(x, g, tm=512):
    m, d = x.shape
    return pl.pallas_call(
        _norm_cast_kernel,
        grid=(m // tm,),
        in_specs=[pl.BlockSpec((tm, d), lambda i: (i, 0)), pl.BlockSpec((1, d), lambda i: (0, 0))],
        out_specs=pl.BlockSpec((tm, d), lambda i: (i, 0)),
        out_shape=jax.ShapeDtypeStruct((m, d), BF16),
        compiler_params=_params(1),
        name="norm_cast",
    )(x, g.reshape(1, d))


def _cast_weights_once(pairs):
    @pl.when(pl.program_id(1) == 0)
    def _():
        for src_ref, dst_ref in pairs:
            dst_ref[...] = src_ref[...].astype(dst_ref.dtype)


def _sc_in_kernel(h_ref, wb_ref, wc_ref, wh_ref, cw_ref, o_ref, wb_bf, wc_bf, wh_bf, carry_ref, *, tiles_per_seq, n_sub):
    _cast_weights_once([(wb_ref, wb_bf), (wc_ref, wc_bf), (wh_ref, wh_bf)])
    carry = jnp.where(pl.program_id(1) % tiles_per_seq == 0, 0.0, carry_ref[...])
    sub = h_ref.shape[0] // n_sub
    cw = cw_ref[...]
    for s in range(n_sub):
        rows = pl.ds(s * sub, sub)
        h = h_ref[rows, :]
        p = _dot(h, wc_bf[...]) * _dot(h, wh_bf[...])
        y = _causal_conv3(p, carry, cw)
        carry = p[sub - 8:, :]
        o_ref[rows, :] = (_dot(h, wb_bf[...]) * y).astype(o_ref.dtype)
    carry_ref[...] = carry


def shortconv_in(h, w_in, conv_w, seq, tm=512, tn=512, n_sub=2):
    m, d = h.shape
    nj = d // tn
    return pl.pallas_call(
        functools.partial(_sc_in_kernel, tiles_per_seq=seq // tm, n_sub=n_sub),
        grid=(nj, m // tm),
        in_specs=[
            pl.BlockSpec((tm, d), lambda j, i: (i, 0)),
            pl.BlockSpec((d, tn), lambda j, i: (0, j)),
            pl.BlockSpec((d, tn), lambda j, i: (0, nj + j)),
            pl.BlockSpec((d, tn), lambda j, i: (0, 2 * nj + j)),
            pl.BlockSpec((3, tn), lambda j, i: (0, j)),
        ],
        out_specs=pl.BlockSpec((tm, tn), lambda j, i: (i, j)),
        out_shape=jax.ShapeDtypeStruct((m, d), BF16),
        scratch_shapes=[pltpu.VMEM((d, tn), BF16)] * 3 + [pltpu.VMEM((8, tn), F32)],
        compiler_params=_params(2),
        name="shortconv_in",
    )(h, w_in, w_in, w_in, conv_w)


def _mm_norm_res_kernel(a_ref, w_ref, res_ref, g1_ref, *rest, n_sub, has_next):
    if has_next:
        g2_ref, x_out, h_out = rest
    else:
        x_out, = rest
    sub = a_ref.shape[0] // n_sub
    for s in range(n_sub):
        rows = pl.ds(s * sub, sub)
        xn = res_ref[rows, :] + _rms(_dot(a_ref[rows, :], w_ref[...]), g1_ref[...])
        x_out[rows, :] = xn
        if has_next:
            h_out[rows, :] = _rms(xn, g2_ref[...]).astype(h_out.dtype)


def mm_norm_res(a, w, res, g1, g2=None, tm=512, n_sub=4):
    m, kdim = a.shape
    d = w.shape[1]
    has_next = g2 is not None
    row = pl.BlockSpec((tm, d), lambda i: (i, 0))
    vec = pl.BlockSpec((1, d), lambda i: (0, 0))
    in_specs = [pl.BlockSpec((tm, kdim), lambda i: (i, 0)),
                pl.BlockSpec((kdim, d), lambda i: (0, 0), pipeline_mode=pl.Buffered(1)), row, vec]
    args = [a, w, res, g1.reshape(1, d)]
    out_specs, out_shape = [row], [jax.ShapeDtypeStruct((m, d), F32)]
    if has_next:
        in_specs.append(vec)
        args.append(g2.reshape(1, d))
        out_specs.append(row)
        out_shape.append(jax.ShapeDtypeStruct((m, d), BF16))
    out = pl.pallas_call(
        functools.partial(_mm_norm_res_kernel, n_sub=n_sub, has_next=has_next),
        grid=(m // tm,),
        in_specs=in_specs,
        out_specs=out_specs,
        out_shape=out_shape,
        compiler_params=_params(1),
        name="mm_norm_res",
    )(*args)
    return tuple(out) if has_next else (out[0], None)


def _ffn_up_kernel(h_ref, wg_ref, wu_ref, cg_ref, cu_ref, bg_ref, bu_ref, o_ref, wg_bf, wu_bf, carry_g, carry_u,
                   *, tiles_per_seq, n_sub):
    _cast_weights_once([(wg_ref, wg_bf), (wu_ref, wu_bf)])
    first = pl.program_id(1) % tiles_per_seq == 0
    cg = jnp.where(first, 0.0, carry_g[...])
    cu = jnp.where(first, 0.0, carry_u[...])
    sub = h_ref.shape[0] // n_sub
    for s in range(n_sub):
        rows = pl.ds(s * sub, sub)
        h = h_ref[rows, :]
        pg = _dot(h, wg_bf[...])
        pu = _dot(h, wu_bf[...])
        gate = _causal_conv3(pg, cg, cg_ref[...]) + bg_ref[...]
        up = _causal_conv3(pu, cu, cu_ref[...]) + bu_ref[...]
        cg = pg[sub - 8:, :]
        cu = pu[sub - 8:, :]
        o_ref[rows, :] = (gate * _sigmoid(gate) * up).astype(o_ref.dtype)
    carry_g[...] = cg
    carry_u[...] = cu


def ffn_up(h, w_up, conv_w, conv_b, seq, tm=1024, tn=512, n_sub=4):
    m, d = h.shape
    f = w_up.shape[1] // 2
    nj = f // tn
    conv_b = conv_b.reshape(1, 2 * f)
    return pl.pallas_call(
        functools.partial(_ffn_up_kernel, tiles_per_seq=seq // tm, n_sub=n_sub),
        grid=(nj, m // tm),
        in_specs=[
            pl.BlockSpec((tm, d), lambda j, i: (i, 0)),
            pl.BlockSpec((d, tn), lambda j, i: (0, j)),
            pl.BlockSpec((d, tn), lambda j, i: (0, nj + j)),
            pl.BlockSpec((3, tn), lambda j, i: (0, j)),
            pl.BlockSpec((3, tn), lambda j, i: (0, nj + j)),
            pl.BlockSpec((1, tn), lambda j, i: (0, j)),
            pl.BlockSpec((1, tn), lambda j, i: (0, nj + j)),
        ],
        out_specs=pl.BlockSpec((tm, tn), lambda j, i: (i, j)),
        out_shape=jax.ShapeDtypeStruct((m, f), BF16),
        scratch_shapes=[pltpu.VMEM((d, tn), BF16)] * 2 + [pltpu.VMEM((8, tn), F32)] * 2,
        compiler_params=_params(2),
        name="ffn_up",
    )(h, w_up, w_up, conv_w, conv_w, conv_b, conv_b)


def _token_shift_delta(h_ref, halo_ref, tiles_per_seq, axis):
    h = h_ref[...].astype(F32)
    prev = halo_ref[BF16_SUBLANES - 1:BF16_SUBLANES, :].astype(F32)
    prev = jnp.where(pl.program_id(axis) % tiles_per_seq == 0, 0.0, prev)
    return h, _shift_rows(h, prev) - h


def _rw_lora_kernel(h_ref, halo_ref, mu_ref, w1_ref, a1_ref, g1_ref, tw_ref, ta_ref, tg_ref, *, tiles_per_seq):
    h, xx = _token_shift_delta(h_ref, halo_ref, tiles_per_seq, 0)
    xw = (h + xx * mu_ref[1:2, :]).astype(BF16)
    xa = (h + xx * mu_ref[4:5, :]).astype(BF16)
    xg = (h + xx * mu_ref[5:6, :]).astype(BF16)
    tw_ref[...] = jnp.tanh(_dot(xw, w1_ref[...])).astype(tw_ref.dtype)
    ta_ref[...] = _dot(xa, a1_ref[...]).astype(ta_ref.dtype)
    tg_ref[...] = _sigmoid(_dot(xg, g1_ref[...])).astype(tg_ref.dtype)


def _halo_spec(tm, d, grid_axis):
    step = tm // BF16_SUBLANES
    if grid_axis == 0:
        return pl.BlockSpec((BF16_SUBLANES, d), lambda i: (jnp.maximum(i * step - 1, 0), 0))
    return pl.BlockSpec((BF16_SUBLANES, d), lambda j, i: (jnp.maximum(i * step - 1, 0), 0))


def rwkv_lora(h, mu, w1, a1, g1, seq, tm=512):
    m, d = h.shape
    full = lambda arr: pl.BlockSpec(arr.shape, lambda i: (0, 0))
    outs = [w1.shape[1], a1.shape[1], g1.shape[1]]
    return pl.pallas_call(
        functools.partial(_rw_lora_kernel, tiles_per_seq=seq // tm),
        grid=(m // tm,),
        in_specs=[pl.BlockSpec((tm, d), lambda i: (i, 0)), _halo_spec(tm, d, 0), full(mu), full(w1), full(a1), full(g1)],
        out_specs=[pl.BlockSpec((tm, n), lambda i: (i, 0)) for n in outs],
        out_shape=[jax.ShapeDtypeStruct((m, n), BF16) for n in outs],
        compiler_params=_params(1),
        name="rwkv_lora",
    )(h, h, mu, w1, a1, g1)


def _rw_proj_kernel(h_ref, halo_ref, mu_ref, wr_ref, wk_ref, wv_ref, tw_ref, ta_ref, tg_ref,
                    w2_ref, a2_ref, g2_ref, w0_ref, a0_ref, kk_ref, ka_ref, bd_ref,
                    r_out, ld_out, k_out, v_out, av_out, bv_out, g_out, *, tiles_per_seq, n_sub):
    h_all, xx_all = _token_shift_delta(h_ref, halo_ref, tiles_per_seq, 1)
    sub = h_ref.shape[0] // n_sub
    for s in range(n_sub):
        rows = pl.ds(s * sub, sub)
        h = h_all[s * sub:(s + 1) * sub, :]
        xx = xx_all[s * sub:(s + 1) * sub, :]
        xr = (h + xx * mu_ref[0:1, :]).astype(BF16)
        xk = (h + xx * mu_ref[2:3, :]).astype(BF16)
        xv = (h + xx * mu_ref[3:4, :]).astype(BF16)
        r = _dot(xr, wr_ref[...])
        k = _dot(xk, wk_ref[...])
        v = _dot(xv, wv_ref[...])
        wl = w0_ref[...] + _dot(tw_ref[rows, :], w2_ref[...])
        z = -wl
        w_log = -(jnp.maximum(z, 0.0) + jnp.log(1.0 + jnp.exp(-jnp.abs(z)))) - 0.5
        ld = -jnp.exp(w_log)
        a = _sigmoid(a0_ref[...] + _dot(ta_ref[rows, :], a2_ref[...]))
        g = _dot(tg_ref[rows, :], g2_ref[...])
        kk = k * kk_ref[...]
        nrm = jnp.sqrt(_head_sum(kk * kk, bd_ref[...]))
        kk = kk / jnp.maximum(nrm, L2_EPS)
        k = k * (1.0 + (a - 1.0) * ka_ref[...])
        r_out[rows, :] = r.astype(r_out.dtype)
        ld_out[rows, :] = ld
        k_out[rows, :] = k.astype(k_out.dtype)
        v_out[rows, :] = v.astype(v_out.dtype)
        av_out[rows, :] = (-kk).astype(av_out.dtype)
        bv_out[rows, :] = (kk * a).astype(bv_out.dtype)
        g_out[rows, :] = g.astype(g_out.dtype)


def rwkv_proj(h, mu, wr, wk, wv, tw, ta, tg, w2, a2, g2, w0, a0, k_k, k_a, bd, seq, tm=512, tn=512, n_sub=2):
    m, d = h.shape
    colw = lambda arr: pl.BlockSpec((arr.shape[0], tn), lambda j, i: (0, j))
    rowt = lambda arr: pl.BlockSpec((tm, arr.shape[1]), lambda j, i: (i, 0))
    vecs = [w0.reshape(1, d), a0.reshape(1, d), k_k.reshape(1, d), k_a.reshape(1, d)]
    out_dtypes = [BF16, F32, BF16, BF16, BF16, BF16, BF16]
    return pl.pallas_call(
        functools.partial(_rw_proj_kernel, tiles_per_seq=seq // tm, n_sub=n_sub),
        grid=(d // tn, m // tm),
        in_specs=[pl.BlockSpec((tm, d), lambda j, i: (i, 0)), _halo_spec(tm, d, 1),
                  pl.BlockSpec(mu.shape, lambda j, i: (0, 0)),
                  colw(wr), colw(wk), colw(wv), rowt(tw), rowt(ta), rowt(tg), colw(w2), colw(a2), colw(g2)]
                 + [colw(x) for x in vecs] + [pl.BlockSpec(bd.shape, lambda j, i: (0, 0))],
        out_specs=[pl.BlockSpec((tm, tn), lambda j, i: (i, j)) for _ in out_dtypes],
        out_shape=[jax.ShapeDtypeStruct((m, d), dt) for dt in out_dtypes],
        compiler_params=_params(2),
        name="rwkv_proj",
    )(h, h, mu, wr, wk, wv, tw, ta, tg, w2, a2, g2, *vecs, bd)


def _stack_pair(x, blk):
    return jnp.where(blk, jnp.concatenate([x, x], axis=0), 0.0).astype(BF16)


def _rw_scan_kernel(r_ref, ld_ref, k_ref, v_ref, av_ref, bv_ref, g_ref, rk_ref, gng_ref, gnb_ref, bd_ref,
                    o_ref, s_ref, *, n_pairs):
    L = CHUNK

    @pl.when(pl.program_id(2) == 0)
    def _():
        s_ref[...] = jnp.zeros_like(s_ref)

    n2 = 2 * L
    row = lax.broadcasted_iota(jnp.int32, (n2, n2), 0)
    col = lax.broadcasted_iota(jnp.int32, (n2, n2), 1)
    same = (row // L) == (col // L)
    strict = same & (row > col)
    incl = same & (row >= col)
    eye = (row == col).astype(F32)
    tr = lax.broadcasted_iota(jnp.int32, (L, L), 0)
    tc = lax.broadcasted_iota(jnp.int32, (L, L), 1)
    tri = (tr >= tc).astype(BF16)
    bd = bd_ref[...]

    pairs = range(n_pairs)
    cols = [slice(p * PAIR, (p + 1) * PAIR) for p in pairs]
    ld = ld_ref[...]
    hi, mid, lo = _split3(ld)
    cum = _dot(tri, hi) + _dot(tri, mid) + _dot(tri, lo)
    e_pos = jnp.exp(cum)
    e_neg = jnp.exp(-cum)
    r = r_ref[...].astype(F32)
    k = k_ref[...].astype(F32)
    v = v_ref[...].astype(F32)
    rt_w = r * e_pos
    kt_w = k * e_neg
    bt_w = bv_ref[...].astype(F32) * e_neg
    at_w = av_ref[...].astype(F32) * jnp.exp(cum - ld)
    rt = [_stack_pair(rt_w[:, c], same) for c in cols]
    kt = [_stack_pair(kt_w[:, c], same) for c in cols]
    bt = [_stack_pair(bt_w[:, c], same) for c in cols]
    at = [_stack_pair(at_w[:, c], same) for c in cols]
    vs = [_stack_pair(v[:, c], same) for c in cols]

    cat0 = lambda xs: jnp.concatenate(xs, axis=0)
    ar = [cat0([at[p], rt[p]]) for p in pairs]
    bk = [cat0([bt[p], kt[p]]) for p in pairs]
    big = [_dot_nt(ar[p], bk[p]) for p in pairs]
    a_ab = [jnp.where(strict, x[:n2, :n2], 0.0) for x in big]
    a_ak = [jnp.where(strict, x[:n2, n2:], 0.0).astype(BF16) for x in big]
    a_rb = [jnp.where(incl, x[n2:, :n2], 0.0).astype(BF16) for x in big]
    a_rk = [jnp.where(incl, x[n2:, n2:], 0.0).astype(BF16) for x in big]

    inv = [eye + a for a in a_ab]
    ab = [a.astype(BF16) for a in a_ab]
    pw = [_dot(x, x) for x in ab]
    for _ in range(L.bit_length() - 3):
        pwb = [x.astype(BF16) for x in pw]
        both = [_dot(cat0([inv[p].astype(BF16), pwb[p]]), pwb[p]) for p in pairs]
        inv = [inv[p] + both[p][:n2, :] for p in pairs]
        pw = [both[p][n2:, :] for p in pairs]
    invb = [(inv[p] + _dot(inv[p].astype(BF16), pw[p].astype(BF16))).astype(BF16) for p in pairs]

    s = [s_ref[p] for p in pairs]
    sb = [x.astype(BF16) for x in s]
    ars = [_dot_nt(ar[p], sb[p]) for p in pairs]
    akv = [_dot(cat0([a_ak[p], a_rk[p]]), vs[p]) for p in pairs]
    ub = [_dot(invb[p], (ars[p][:n2, :] + akv[p][:n2, :]).astype(BF16)).astype(BF16) for p in pairs]
    y2 = [ars[p][n2:, :] + akv[p][n2:, :] + _dot(a_rb[p], ub[p]) for p in pairs]
    w_last = e_pos[L - 1:L, :]
    for p in pairs:
        s_ref[p] = (s[p] + _dot_tn(cat0([ub[p], vs[p]]), bk[p])) * w_last[:, cols[p]]
    y = jnp.concatenate([x[:L, :] + x[L:, :] for x in y2], axis=1)

    mean = _head_sum(y, bd) * (1.0 / HEAD_SIZE)
    yc = y - mean
    var = _head_sum(yc * yc, bd) * (1.0 / HEAD_SIZE)
    yn = yc * lax.rsqrt(var + GN_EPS) * gng_ref[...] + gnb_ref[...]
    bonus = _head_sum(r * k * rk_ref[...], bd) * v
    o_ref[...] = ((yn + bonus) * g_ref[...].astype(F32)).astype(o_ref.dtype)


def rwkv_scan(r, ld, k, v, av, bv, g, r_k, gn_g, gn_b, bd, seq, n_pairs=16):
    m, d = r.shape
    n_pairs = min(n_pairs, d // PAIR)
    cw = n_pairs * PAIR
    nc = seq // CHUNK
    blk = pl.BlockSpec((CHUNK, cw), lambda b, j, c: (b * nc + c, j))
    vec = pl.BlockSpec((1, cw), lambda b, j, c: (0, j))
    return pl.pallas_call(
        functools.partial(_rw_scan_kernel, n_pairs=n_pairs),
        grid=(m // seq, d // cw, nc),
        in_specs=[blk] * 7 + [vec] * 3 + [pl.BlockSpec(bd.shape, lambda b, j, c: (0, 0))],
        out_specs=blk,
        out_shape=jax.ShapeDtypeStruct((m, d), BF16),
        scratch_shapes=[pltpu.VMEM((n_pairs, PAIR, PAIR), F32)],
        compiler_params=_params(3),
        name="rwkv_scan",
    )(r, ld, k, v, av, bv, g, r_k.reshape(1, d), gn_g.reshape(1, d), gn_b.reshape(1, d), bd)


def _pad_cols(w, n):
    return jnp.pad(w, ((0, 0), (0, n - w.shape[1])))


def _pad_rows(w, n):
    return jnp.pad(w, ((0, n - w.shape[0]), (0, 0)))


def kernel(x, norm_g, sc_w_in, sc_conv, sc_w_out, rw_mu, rw_wr, rw_wk, rw_wv, rw_wo, rw_w0, rw_w1, rw_w2,
           rw_a0, rw_a1, rw_a2, rw_g1, rw_g2, rw_kk, rw_ka, rw_rk, rw_gn_g, rw_gn_b,
           ffn_w_up, ffn_conv, ffn_conv_b, ffn_w_down):
    bsz, seq, d = x.shape
    depth = norm_g.shape[0]
    xf = x.reshape(bsz * seq, d)
    lane = jnp.arange(LANES)
    bd = (lane[:, None] // HEAD_SIZE == lane[None, :] // HEAD_SIZE).astype(BF16)
    bf = lambda w: w.astype(BF16)

    h = norm_cast(xf, norm_g[0, 0])
    for i in range(depth):
        j = i // 2
        if i % 2 == 0:
            mix = shortconv_in(h, sc_w_in[j], sc_conv[j], seq)
            w_mix_out = bf(sc_w_out[j])
        else:
            lora = LANES * pl.cdiv(rw_w1.shape[2], LANES)
            tw, ta, tg = rwkv_lora(h, rw_mu[j], bf(_pad_cols(rw_w1[j], lora)), bf(_pad_cols(rw_a1[j], lora)),
                                   bf(rw_g1[j]), seq)
            parts = rwkv_proj(h, rw_mu[j], bf(rw_wr[j]), bf(rw_wk[j]), bf(rw_wv[j]), tw, ta, tg,
                              bf(_pad_rows(rw_w2[j], lora)), bf(_pad_rows(rw_a2[j], lora)), bf(rw_g2[j]),
                              rw_w0[j], rw_a0[j], rw_kk[j], rw_ka[j], bd, seq)
            mix = rwkv_scan(*parts, rw_rk[j], rw_gn_g[j], rw_gn_b[j], bd, seq)
            w_mix_out = bf(rw_wo[j])
        xf, h = mm_norm_res(mix, w_mix_out, xf, norm_g[i, 1], norm_g[i, 2], tm=512, n_sub=4)
        act = ffn_up(h, ffn_w_up[i], ffn_conv[i], ffn_conv_b[i], seq)
        g_next = norm_g[i + 1, 0] if i + 1 < depth else None
        xf, h = mm_norm_res(act, bf(ffn_w_down[i]), xf, norm_g[i, 3], g_next, tm=256, n_sub=2)
    return xf.reshape(bsz, seq, d)
```

```python
import functools

import jax
import jax.numpy as jnp
from jax import lax
from jax.experimental import pallas as pl
from jax.experimental.pallas import tpu as pltpu

F32 = jnp.float32
BF16 = jnp.bfloat16

HEAD_SIZE = 64
RMS_EPS = 1e-6
GN_EPS = 64e-5
L2_EPS = 1e-12

LANES = 128
BF16_SUBLANES = 16
VMEM_LIMIT_BYTES = 56 * 1024 * 1024
CHUNK = 64
PAIR = 2 * HEAD_SIZE


def _params(n_axes):
    return pltpu.CompilerParams(dimension_semantics=("arbitrary",) * n_axes,
                                vmem_limit_bytes=VMEM_LIMIT_BYTES)


def _dot(a, b):
    return jnp.dot(a, b, preferred_element_type=F32)


def _dot_nt(a, b):
    return lax.dot_general(a, b, (((1,), (1,)), ((), ())), preferred_element_type=F32)


def _dot_tn(a, b):
    return lax.dot_general(a, b, (((0,), (0,)), ((), ())), preferred_element_type=F32)


def _split2(x):
    hi = x.astype(BF16)
    lo = (x - hi.astype(F32)).astype(BF16)
    return hi, lo


def _split3(x):
    hi = x.astype(BF16)
    r1 = x - hi.astype(F32)
    mid = r1.astype(BF16)
    lo = (r1 - mid.astype(F32)).astype(BF16)
    return hi, mid, lo


def _rms(y, g):
    return y * lax.rsqrt(jnp.mean(y * y, axis=-1, keepdims=True) + RMS_EPS) * g


def _sigmoid(x):
    return 1.0 / (1.0 + jnp.exp(-x))


def _head_sum(x, bd):
    rows, n = x.shape[0], x.shape[1] // LANES
    if n > 1:
        x = jnp.concatenate([x[:, s * LANES:(s + 1) * LANES] for s in range(n)], axis=0)
    hi, lo = _split2(x)
    out = _dot(jnp.concatenate([hi, lo], axis=1), jnp.concatenate([bd, bd], axis=0))
    if n > 1:
        out = jnp.concatenate([out[s * rows:(s + 1) * rows, :] for s in range(n)], axis=1)
    return out


def _shift_rows(p, prev1, prev2=None):
    row = lax.broadcasted_iota(jnp.int32, p.shape, 0)
    p1 = jnp.where(row == 0, prev1, pltpu.roll(p, 1, 0))
    if prev2 is None:
        return p1
    p2 = jnp.where(row == 0, prev2, jnp.where(row == 1, prev1, pltpu.roll(p, 2, 0)))
    return p1, p2


def _causal_conv3(p, carry, w):
    p1, p2 = _shift_rows(p, carry[7:8, :], carry[6:7, :])
    return w[0:1, :] * p2 + w[1:2, :] * p1 + w[2:3, :] * p


def _norm_cast_kernel(x_ref, g_ref, o_ref):
    o_ref[...] = _rms(x_ref[...], g_ref[...]).astype(o_ref.dtype)


def norm_cast(x, g, tm=512):
    m, d = x.shape
    return pl.pallas_call(
        _norm_cast_kernel,
        grid=(m // tm,),
        in_specs=[pl.BlockSpec((tm, d), lambda i: (i, 0)), pl.BlockSpec((1, d), lambda i: (0, 0))],
        out_specs=pl.BlockSpec((tm, d), lambda i: (i, 0)),
        out_shape=jax.ShapeDtypeStruct((m, d), BF16),
        compiler_params=_params(1),
        name="norm_cast",
    )(x, g.reshape(1, d))


def _cast_weights_once(pairs):
    @pl.when(pl.program_id(1) == 0)
    def _():
        for src_ref, dst_ref in pairs:
            dst_ref[...] = src_ref[...].astype(dst_ref.dtype)


def _sc_in_kernel(h_ref, wb_ref, wc_ref, wh_ref, cw_ref, o_ref, wb_bf, wc_bf, wh_bf, carry_ref, *, tiles_per_seq, n_sub):
    _cast_weights_once([(wb_ref, wb_bf), (wc_ref, wc_bf), (wh_ref, wh_bf)])
    carry = jnp.where(pl.program_id(1) % tiles_per_seq == 0, 0.0, carry_ref[...])
    sub = h_ref.shape[0] // n_sub
    cw = cw_ref[...]
    for s in range(n_sub):
        rows = pl.ds(s * sub, sub)
        h = h_ref[rows, :]
        p = _dot(h, wc_bf[...]) * _dot(h, wh_bf[...])
        y = _causal_conv3(p, carry, cw)
        carry = p[sub - 8:, :]
        o_ref[rows, :] = (_dot(h, wb_bf[...]) * y).astype(o_ref.dtype)
    carry_ref[...] = carry


def shortconv_in(h, w_in, layer, conv_w, seq, tm=1024, tn=512, n_sub=1):
    m, d = h.shape
    nj = d // tn
    wspec = lambda off: pl.BlockSpec((None, d, tn), lambda j, i: (layer, 0, off * nj + j))
    return pl.pallas_call(
        functools.partial(_sc_in_kernel, tiles_per_seq=seq // tm, n_sub=n_sub),
        grid=(nj, m // tm),
        in_specs=[
            pl.BlockSpec((tm, d), lambda j, i: (i, 0)),
            wspec(0), wspec(1), wspec(2),
            pl.BlockSpec((3, tn), lambda j, i: (0, j)),
        ],
        out_specs=pl.BlockSpec((tm, tn), lambda j, i: (i, j)),
        out_shape=jax.ShapeDtypeStruct((m, d), BF16),
        scratch_shapes=[pltpu.VMEM((d, tn), BF16)] * 3 + [pltpu.VMEM((8, tn), F32)],
        compiler_params=_params(2),
        name="shortconv_in",
    )(h, w_in, w_in, w_in, conv_w)


def _mm_norm_res_kernel(a_ref, w_ref, res_ref, g1_ref, *rest, n_sub, has_next):
    if has_next:
        g2_ref, x_out, h_out = rest
    else:
        x_out, = rest
    sub = a_ref.shape[0] // n_sub
    for s in range(n_sub):
        rows = pl.ds(s * sub, sub)
        xn = res_ref[rows, :] + _rms(_dot(a_ref[rows, :], w_ref[...]), g1_ref[...])
        x_out[rows, :] = xn
        if has_next:
            h_out[rows, :] = _rms(xn, g2_ref[...]).astype(h_out.dtype)


def mm_norm_res(a, w, res, g1, g2=None, tm=512, n_sub=4):
    m, kdim = a.shape
    d = w.shape[1]
    has_next = g2 is not None
    row = pl.BlockSpec((tm, d), lambda i: (i, 0))
    vec = pl.BlockSpec((1, d), lambda i: (0, 0))
    in_specs = [pl.BlockSpec((tm, kdim), lambda i: (i, 0)),
                pl.BlockSpec((kdim, d), lambda i: (0, 0), pipeline_mode=pl.Buffered(1)), row, vec]
    args = [a, w, res, g1.reshape(1, d)]
    out_specs, out_shape = [row], [jax.ShapeDtypeStruct((m, d), F32)]
    if has_next:
        in_specs.append(vec)
        args.append(g2.reshape(1, d))
        out_specs.append(row)
        out_shape.append(jax.ShapeDtypeStruct((m, d), BF16))
    out = pl.pallas_call(
        functools.partial(_mm_norm_res_kernel, n_sub=n_sub, has_next=has_next),
        grid=(m // tm,),
        in_specs=in_specs,
        out_specs=out_specs,
        out_shape=out_shape,
        compiler_params=_params(1),
        name="mm_norm_res",
    )(*args)
    return tuple(out) if has_next else (out[0], None)


def _ffn_up_kernel(h_ref, wg_ref, wu_ref, cg_ref, cu_ref, bg_ref, bu_ref, o_ref, wg_bf, wu_bf, carry_g, carry_u,
                   *, tiles_per_seq, n_sub):
    _cast_weights_once([(wg_ref, wg_bf), (wu_ref, wu_bf)])
    first = pl.program_id(1) % tiles_per_seq == 0
    cg = jnp.where(first, 0.0, carry_g[...])
    cu = jnp.where(first, 0.0, carry_u[...])
    sub = h_ref.shape[0] // n_sub
    for s in range(n_sub):
        rows = pl.ds(s * sub, sub)
        h = h_ref[rows, :]
        pg = _dot(h, wg_bf[...])
        pu = _dot(h, wu_bf[...])
        gate = _causal_conv3(pg, cg, cg_ref[...]) + bg_ref[...]
        up = _causal_conv3(pu, cu, cu_ref[...]) + bu_ref[...]
        cg = pg[sub - 8:, :]
        cu = pu[sub - 8:, :]
        o_ref[rows, :] = (gate * _sigmoid(gate) * up).astype(o_ref.dtype)
    carry_g[...] = cg
    carry_u[...] = cu


def ffn_up(h, w_up, layer, conv_w, conv_b, seq, tm=1024, tn=512, n_sub=1):
    m, d = h.shape
    f = w_up.shape[2] // 2
    nj = f // tn
    conv_b = conv_b.reshape(1, 2 * f)
    wspec = lambda off: pl.BlockSpec((None, d, tn), lambda j, i: (layer, 0, off * nj + j))
    return pl.pallas_call(
        functools.partial(_ffn_up_kernel, tiles_per_seq=seq // tm, n_sub=n_sub),
        grid=(nj, m // tm),
        in_specs=[
            pl.BlockSpec((tm, d), lambda j, i: (i, 0)),
            wspec(0), wspec(1),
            pl.BlockSpec((3, tn), lambda j, i: (0, j)),
            pl.BlockSpec((3, tn), lambda j, i: (0, nj + j)),
            pl.BlockSpec((1, tn), lambda j, i: (0, j)),
            pl.BlockSpec((1, tn), lambda j, i: (0, nj + j)),
        ],
        out_specs=pl.BlockSpec((tm, tn), lambda j, i: (i, j)),
        out_shape=jax.ShapeDtypeStruct((m, f), BF16),
        scratch_shapes=[pltpu.VMEM((d, tn), BF16)] * 2 + [pltpu.VMEM((8, tn), F32)] * 2,
        compiler_params=_params(2),
        name="ffn_up",
    )(h, w_up, w_up, conv_w, conv_w, conv_b, conv_b)


def _token_shift_delta(h_ref, halo_ref, tiles_per_seq, axis):
    h = h_ref[...].astype(F32)
    prev = halo_ref[BF16_SUBLANES - 1:BF16_SUBLANES, :].astype(F32)
    prev = jnp.where(pl.program_id(axis) % tiles_per_seq == 0, 0.0, prev)
    return h, _shift_rows(h, prev) - h


def _rw_lora_kernel(h_ref, halo_ref, mu_ref, w1_ref, a1_ref, g1_ref, tw_ref, ta_ref, tg_ref, *, tiles_per_seq):
    h, xx = _token_shift_delta(h_ref, halo_ref, tiles_per_seq, 0)
    xw = (h + xx * mu_ref[1:2, :]).astype(BF16)
    xa = (h + xx * mu_ref[4:5, :]).astype(BF16)
    xg = (h + xx * mu_ref[5:6, :]).astype(BF16)
    tw_ref[...] = jnp.tanh(_dot(xw, w1_ref[...])).astype(tw_ref.dtype)
    ta_ref[...] = _dot(xa, a1_ref[...]).astype(ta_ref.dtype)
    tg_ref[...] = _sigmoid(_dot(xg, g1_ref[...])).astype(tg_ref.dtype)


def _halo_spec(tm, d, grid_axis):
    step = tm // BF16_SUBLANES
    if grid_axis == 0:
        return pl.BlockSpec((BF16_SUBLANES, d), lambda i: (jnp.maximum(i * step - 1, 0), 0))
    return pl.BlockSpec((BF16_SUBLANES, d), lambda j, i: (jnp.maximum(i * step - 1, 0), 0))


def rwkv_lora(h, mu, w1, a1, g1, seq, tm=512):
    m, d = h.shape
    full = lambda arr: pl.BlockSpec(arr.shape, lambda i: (0, 0))
    outs = [w1.shape[1], a1.shape[1], g1.shape[1]]
    return pl.pallas_call(
        functools.partial(_rw_lora_kernel, tiles_per_seq=seq // tm),
        grid=(m // tm,),
        in_specs=[pl.BlockSpec((tm, d), lambda i: (i, 0)), _halo_spec(tm, d, 0), full(mu), full(w1), full(a1), full(g1)],
        out_specs=[pl.BlockSpec((tm, n), lambda i: (i, 0)) for n in outs],
        out_shape=[jax.ShapeDtypeStruct((m, n), BF16) for n in outs],
        compiler_params=_params(1),
        name="rwkv_lora",
    )(h, h, mu, w1, a1, g1)


def _rw_proj_kernel(h_ref, halo_ref, mu_ref, wr_ref, wk_ref, wv_ref, tw_ref, ta_ref, tg_ref,
                    w2_ref, a2_ref, g2_ref, w0_ref, a0_ref, kk_ref, ka_ref, bd_ref,
                    r_out, ld_out, k_out, v_out, av_out, bv_out, g_out, *, tiles_per_seq, n_sub):
    h_all, xx_all = _token_shift_delta(h_ref, halo_ref, tiles_per_seq, 1)
    sub = h_ref.shape[0] // n_sub
    for s in range(n_sub):
        rows = pl.ds(s * sub, sub)
        h = h_all[s * sub:(s + 1) * sub, :]
        xx = xx_all[s * sub:(s + 1) * sub, :]
        xr = (h + xx * mu_ref[0:1, :]).astype(BF16)
        xk = (h + xx * mu_ref[2:3, :]).astype(BF16)
        xv = (h + xx * mu_ref[3:4, :]).astype(BF16)
        r = _dot(xr, wr_ref[...])
        k = _dot(xk, wk_ref[...])
        v = _dot(xv, wv_ref[...])
        wl = w0_ref[...] + _dot(tw_ref[rows, :], w2_ref[...])
        z = -wl
        w_log = -(jnp.maximum(z, 0.0) + jnp.log(1.0 + jnp.exp(-jnp.abs(z)))) - 0.5
        ld = -jnp.exp(w_log)
        a = _sigmoid(a0_ref[...] + _dot(ta_ref[rows, :], a2_ref[...]))
        g = _dot(tg_ref[rows, :], g2_ref[...])
        kk = k * kk_ref[...]
        nrm = jnp.sqrt(_head_sum(kk * kk, bd_ref[...]))
        kk = kk / jnp.maximum(nrm, L2_EPS)
        k = k * (1.0 + (a - 1.0) * ka_ref[...])
        r_out[rows, :] = r.astype(r_out.dtype)
        ld_out[rows, :] = ld
        k_out[rows, :] = k.astype(k_out.dtype)
        v_out[rows, :] = v.astype(v_out.dtype)
        av_out[rows, :] = (-kk).astype(av_out.dtype)
        bv_out[rows, :] = (kk * a).astype(bv_out.dtype)
        g_out[rows, :] = g.astype(g_out.dtype)


def rwkv_proj(h, mu, wr, wk, wv, tw, ta, tg, w2, a2, g2, w0, a0, k_k, k_a, bd, seq, tm=512, tn=512, n_sub=2):
    m, d = h.shape
    colw = lambda arr: pl.BlockSpec((arr.shape[0], tn), lambda j, i: (0, j))
    rowt = lambda arr: pl.BlockSpec((tm, arr.shape[1]), lambda j, i: (i, 0))
    vecs = [w0.reshape(1, d), a0.reshape(1, d), k_k.reshape(1, d), k_a.reshape(1, d)]
    out_dtypes = [BF16, F32, BF16, BF16, BF16, BF16, BF16]
    return pl.pallas_call(
        functools.partial(_rw_proj_kernel, tiles_per_seq=seq // tm, n_sub=n_sub),
        grid=(d // tn, m // tm),
        in_specs=[pl.BlockSpec((tm, d), lambda j, i: (i, 0)), _halo_spec(tm, d, 1),
                  pl.BlockSpec(mu.shape, lambda j, i: (0, 0)),
                  colw(wr), colw(wk), colw(wv), rowt(tw), rowt(ta), rowt(tg), colw(w2), colw(a2), colw(g2)]
                 + [colw(x) for x in vecs] + [pl.BlockSpec(bd.shape, lambda j, i: (0, 0))],
        out_specs=[pl.BlockSpec((tm, tn), lambda j, i: (i, j)) for _ in out_dtypes],
        out_shape=[jax.ShapeDtypeStruct((m, d), dt) for dt in out_dtypes],
        compiler_params=_params(2),
        name="rwkv_proj",
    )(h, h, mu, wr, wk, wv, tw, ta, tg, w2, a2, g2, *vecs, bd)


def _stack_pair(x, blk):
    return jnp.where(blk, jnp.concatenate([x, x], axis=0), 0.0).astype(BF16)


def _rw_scan_kernel(r_ref, ld_ref, k_ref, v_ref, av_ref, bv_ref, g_ref, rk_ref, gng_ref, gnb_ref, bd_ref,
                    o_ref, s_ref, *, n_pairs):
    L = CHUNK

    @pl.when(pl.program_id(2) == 0)
    def _():
        s_ref[...] = jnp.zeros_like(s_ref)

    n2 = 2 * L
    row = lax.broadcasted_iota(jnp.int32, (n2, n2), 0)
    col = lax.broadcasted_iota(jnp.int32, (n2, n2), 1)
    same = (row // L) == (col // L)
    strict = same & (row > col)
    incl = same & (row >= col)
    eye = (row == col).astype(F32)
    tr = lax.broadcasted_iota(jnp.int32, (L, 4 * L), 0)
    tc = lax.broadcasted_iota(jnp.int32, (L, 4 * L), 1)
    tri = ((tr >= tc % L) & (tc < 3 * L)).astype(BF16)
    bd = bd_ref[...]

    pairs = range(n_pairs)
    cols = [slice(p * PAIR, (p + 1) * PAIR) for p in pairs]
    ld = ld_ref[...]
    hi, mid, lo = _split3(ld)
    cum = _dot(tri, jnp.concatenate([hi, mid, lo, jnp.zeros_like(lo)], axis=0))
    e_pos = jnp.exp(cum)
    e_neg = jnp.exp(-cum)
    r = r_ref[...].astype(F32)
    k = k_ref[...].astype(F32)
    v = v_ref[...].astype(F32)
    rt_w = r * e_pos
    kt_w = k * e_neg
    bt_w = bv_ref[...].astype(F32) * e_neg
    at_w = av_ref[...].astype(F32) * jnp.exp(cum - ld)
    rt = [_stack_pair(rt_w[:, c], same) for c in cols]
    kt = [_stack_pair(kt_w[:, c], same) for c in cols]
    bt = [_stack_pair(bt_w[:, c], same) for c in cols]
    at = [_stack_pair(at_w[:, c], same) for c in cols]
    vs = [_stack_pair(v[:, c], same) for c in cols]

    cat0 = lambda xs: jnp.concatenate(xs, axis=0)
    ar = [cat0([at[p], rt[p]]) for p in pairs]
    bk = [cat0([bt[p], kt[p]]) for p in pairs]
    big = [_dot_nt(ar[p], bk[p]) for p in pairs]
    a_ab = [jnp.where(strict, x[:n2, :n2], 0.0) for x in big]
    a_ak = [jnp.where(strict, x[:n2, n2:], 0.0).astype(BF16) for x in big]
    a_rb = [jnp.where(incl, x[n2:, :n2], 0.0).astype(BF16) for x in big]
    a_rk = [jnp.where(incl, x[n2:, n2:], 0.0).astype(BF16) for x in big]

    inv = [eye + a for a in a_ab]
    ab = [a.astype(BF16) for a in a_ab]
    pw = [_dot(x, x) for x in ab]
    for _ in range(L.bit_length() - 3):
        pwb = [x.astype(BF16) for x in pw]
        both = [_dot(cat0([inv[p].astype(BF16), pwb[p]]), pwb[p]) for p in pairs]
        inv = [inv[p] + both[p][:n2, :] for p in pairs]
        pw = [both[p][n2:, :] for p in pairs]
    invb = [(inv[p] + _dot(inv[p].astype(BF16), pw[p].astype(BF16))).astype(BF16) for p in pairs]

    s = [s_ref[p] for p in pairs]
    sb = [x.astype(BF16) for x in s]
    ars = [_dot_nt(ar[p], sb[p]) for p in pairs]
    akv = [_dot(cat0([a_ak[p], a_rk[p]]), vs[p]) for p in pairs]
    ub = [_dot(invb[p], (ars[p][:n2, :] + akv[p][:n2, :]).astype(BF16)).astype(BF16) for p in pairs]
    y2 = [ars[p][n2:, :] + akv[p][n2:, :] + _dot(a_rb[p], ub[p]) for p in pairs]
    w_last = e_pos[L - 1:L, :]
    for p in pairs:
        s_ref[p] = (s[p] + _dot_tn(cat0([ub[p], vs[p]]), bk[p])) * w_last[:, cols[p]]
    y = jnp.concatenate([x[:L, :] + x[L:, :] for x in y2], axis=1)

    mean = _head_sum(y, bd) * (1.0 / HEAD_SIZE)
    yc = y - mean
    var = _head_sum(yc * yc, bd) * (1.0 / HEAD_SIZE)
    yn = yc * lax.rsqrt(var + GN_EPS) * gng_ref[...] + gnb_ref[...]
    bonus = _head_sum(r * k * rk_ref[...], bd) * v
    o_ref[...] = ((yn + bonus) * g_ref[...].astype(F32)).astype(o_ref.dtype)


def rwkv_scan(r, ld, k, v, av, bv, g, r_k, gn_g, gn_b, bd, seq, n_pairs=16):
    m, d = r.shape
    n_pairs = min(n_pairs, d // PAIR)
    cw = n_pairs * PAIR
    nc = seq // CHUNK
    blk = pl.BlockSpec((CHUNK, cw), lambda b, j, c: (b * nc + c, j))
    vec = pl.BlockSpec((1, cw), lambda b, j, c: (0, j))
    return pl.pallas_call(
        functools.partial(_rw_scan_kernel, n_pairs=n_pairs),
        grid=(m // seq, d // cw, nc),
        in_specs=[blk] * 7 + [vec] * 3 + [pl.BlockSpec(bd.shape, lambda b, j, c: (0, 0))],
        out_specs=blk,
        out_shape=jax.ShapeDtypeStruct((m, d), BF16),
        scratch_shapes=[pltpu.VMEM((n_pairs, PAIR, PAIR), F32)],
        compiler_params=_params(3),
        name="rwkv_scan",
    )(r, ld, k, v, av, bv, g, r_k.reshape(1, d), gn_g.reshape(1, d), gn_b.reshape(1, d), bd)


def _pad_cols(w, n):
    return jnp.pad(w, ((0, 0), (0, n - w.shape[1])))


def _pad_rows(w, n):
    return jnp.pad(w, ((0, n - w.shape[0]), (0, 0)))


def kernel(x, norm_g, sc_w_in, sc_conv, sc_w_out, rw_mu, rw_wr, rw_wk, rw_wv, rw_wo, rw_w0, rw_w1, rw_w2,
           rw_a0, rw_a1, rw_a2, rw_g1, rw_g2, rw_kk, rw_ka, rw_rk, rw_gn_g, rw_gn_b,
           ffn_w_up, ffn_conv, ffn_conv_b, ffn_w_down):
    bsz, seq, d = x.shape
    depth = norm_g.shape[0]
    xf = x.reshape(bsz * seq, d)
    lane = jnp.arange(LANES)
    bd = (lane[:, None] // HEAD_SIZE == lane[None, :] // HEAD_SIZE).astype(BF16)
    bf = lambda w: w.astype(BF16)

    h = norm_cast(xf, norm_g[0, 0])
    for i in range(depth):
        j = i // 2
        if i % 2 == 0:
            mix = shortconv_in(h, sc_w_in, j, sc_conv[j], seq)
            w_mix_out = bf(sc_w_out[j])
        else:
            lora = LANES * pl.cdiv(rw_w1.shape[2], LANES)
            tw, ta, tg = rwkv_lora(h, rw_mu[j], bf(_pad_cols(rw_w1[j], lora)), bf(_pad_cols(rw_a1[j], lora)),
                                   bf(rw_g1[j]), seq)
            parts = rwkv_proj(h, rw_mu[j], bf(rw_wr[j]), bf(rw_wk[j]), bf(rw_wv[j]), tw, ta, tg,
                              bf(_pad_rows(rw_w2[j], lora)), bf(_pad_rows(rw_a2[j], lora)), bf(rw_g2[j]),
                              rw_w0[j], rw_a0[j], rw_kk[j], rw_ka[j], bd, seq)
            mix = rwkv_scan(*parts, rw_rk[j], rw_gn_g[j], rw_gn_b[j], bd, seq)
            w_mix_out = bf(rw_wo[j])
        xf, h = mm_norm_res(mix, w_mix_out, xf, norm_g[i, 1], norm_g[i, 2], tm=512, n_sub=4)
        act = ffn_up(h, ffn_w_up, i, ffn_conv[i], ffn_conv_b[i], seq)
        g_next = norm_g[i + 1, 0] if i + 1 < depth else None
        xf, h = mm_norm_res(act, bf(ffn_w_down[i]), xf, norm_g[i, 3], g_next, tm=256, n_sub=2)
    return xf.reshape(bsz, seq, d)
```

```python
import functools

import jax
import jax.numpy as jnp
from jax import lax
from jax.experimental import pallas as pl
from jax.experimental.pallas import tpu as pltpu

F32 = jnp.float32
BF16 = jnp.bfloat16

HEAD_SIZE = 64
RMS_EPS = 1e-6
GN_EPS = 64e-5
L2_EPS = 1e-12

LANES = 128
BF16_SUBLANES = 16
VMEM_LIMIT_BYTES = 56 * 1024 * 1024
CAST_STEP_BYTES = 4 * 1024 * 1024
CHUNK = 64
PAIR = 2 * HEAD_SIZE


def _params(n_axes):
    return pltpu.CompilerParams(dimension_semantics=("arbitrary",) * n_axes,
                                vmem_limit_bytes=VMEM_LIMIT_BYTES)


def _dot(a, b):
    return jnp.dot(a, b, preferred_element_type=F32)


def _dot_nt(a, b):
    return lax.dot_general(a, b, (((1,), (1,)), ((), ())), preferred_element_type=F32)


def _dot_tn(a, b):
    return lax.dot_general(a, b, (((0,), (0,)), ((), ())), preferred_element_type=F32)


def _split2(x):
    hi = x.astype(BF16)
    lo = (x - hi.astype(F32)).astype(BF16)
    return hi, lo


def _split3(x):
    hi = x.astype(BF16)
    r1 = x - hi.astype(F32)
    mid = r1.astype(BF16)
    lo = (r1 - mid.astype(F32)).astype(BF16)
    return hi, mid, lo


def _rms(y, g):
    return y * lax.rsqrt(jnp.mean(y * y, axis=-1, keepdims=True) + RMS_EPS) * g


def _sigmoid(x):
    return 1.0 / (1.0 + jnp.exp(-x))


def _head_sum(x, bd):
    rows, n = x.shape[0], x.shape[1] // LANES
    if n > 1:
        x = jnp.concatenate([x[:, s * LANES:(s + 1) * LANES] for s in range(n)], axis=0)
    hi, lo = _split2(x)
    out = _dot(jnp.concatenate([hi, lo], axis=1), jnp.concatenate([bd, bd], axis=0))
    if n > 1:
        out = jnp.concatenate([out[s * rows:(s + 1) * rows, :] for s in range(n)], axis=1)
    return out


def _shift_rows(p, prev1, prev2=None):
    row = lax.broadcasted_iota(jnp.int32, p.shape, 0)
    p1 = jnp.where(row == 0, prev1, pltpu.roll(p, 1, 0))
    if prev2 is None:
        return p1
    p2 = jnp.where(row == 0, prev2, jnp.where(row == 1, prev1, pltpu.roll(p, 2, 0)))
    return p1, p2


def _causal_conv3(p, carry, w):
    p1, p2 = _shift_rows(p, carry[7:8, :], carry[6:7, :])
    return w[0:1, :] * p2 + w[1:2, :] * p1 + w[2:3, :] * p


def _norm_cast_kernel(x_ref, g_ref, o_ref):
    o_ref[...] = _rms(x_ref[...], g_ref[...]).astype(o_ref.dtype)


def norm_cast(x, g, tm=512):
    m, d = x.shape
    return pl.pallas_call(
        _norm_cast_kernel,
        grid=(m // tm,),
        in_specs=[pl.BlockSpec((tm, d), lambda i: (i, 0)), pl.BlockSpec((1, d), lambda i: (0, 0))],
        out_specs=pl.BlockSpec((tm, d), lambda i: (i, 0)),
        out_shape=jax.ShapeDtypeStruct((m, d), BF16),
        compiler_params=_params(1),
        name="norm_cast",
    )(x, g.reshape(1, d))


def _cast_kernel(*refs):
    n = len(refs) // 2
    for src_ref, dst_ref in zip(refs[:n], refs[n:]):
        dst_ref[...] = src_ref[...].astype(dst_ref.dtype)


def cast_bf16(ws):
    rows, cols = ws[0].shape
    tm = CAST_STEP_BYTES // (len(ws) * cols * 4)
    tm = min(rows, 1 << (tm.bit_length() - 1))
    spec = pl.BlockSpec((tm, cols), lambda i: (i, 0))
    return pl.pallas_call(
        _cast_kernel,
        grid=(rows // tm,),
        in_specs=[spec] * len(ws),
        out_specs=[spec] * len(ws),
        out_shape=[jax.ShapeDtypeStruct((rows, cols), BF16)] * len(ws),
        compiler_params=_params(1),
        name="cast_bf16",
    )(*ws)


def _cast_weights_once(pairs):
    @pl.when(pl.program_id(1) == 0)
    def _():
        for src_ref, dst_ref in pairs:
            dst_ref[...] = src_ref[...].astype(dst_ref.dtype)


def _sc_in_kernel(h_ref, wb_ref, wc_ref, wh_ref, cw_ref, o_ref, wb_bf, wc_bf, wh_bf, carry_ref, *, tiles_per_seq, n_sub):
    _cast_weights_once([(wb_ref, wb_bf), (wc_ref, wc_bf), (wh_ref, wh_bf)])
    carry = jnp.where(pl.program_id(1) % tiles_per_seq == 0, 0.0, carry_ref[...])
    sub = h_ref.shape[0] // n_sub
    cw = cw_ref[...]
    for s in range(n_sub):
        rows = pl.ds(s * sub, sub)
        h = h_ref[rows, :]
        p = _dot(h, wc_bf[...]) * _dot(h, wh_bf[...])
        y = _causal_conv3(p, carry, cw)
        carry = p[sub - 8:, :]
        o_ref[rows, :] = (_dot(h, wb_bf[...]) * y).astype(o_ref.dtype)
    carry_ref[...] = carry


def shortconv_in(h, w_in, layer, conv_w, seq, tm=1024, tn=512, n_sub=1):
    m, d = h.shape
    nj = d // tn
    wspec = lambda off: pl.BlockSpec((None, d, tn), lambda j, i: (layer, 0, off * nj + j))
    return pl.pallas_call(
        functools.partial(_sc_in_kernel, tiles_per_seq=seq // tm, n_sub=n_sub),
        grid=(nj, m // tm),
        in_specs=[
            pl.BlockSpec((tm, d), lambda j, i: (i, 0)),
            wspec(0), wspec(1), wspec(2),
            pl.BlockSpec((3, tn), lambda j, i: (0, j)),
        ],
        out_specs=pl.BlockSpec((tm, tn), lambda j, i: (i, j)),
        out_shape=jax.ShapeDtypeStruct((m, d), BF16),
        scratch_shapes=[pltpu.VMEM((d, tn), BF16)] * 3 + [pltpu.VMEM((8, tn), F32)],
        compiler_params=_params(2),
        name="shortconv_in",
    )(h, w_in, w_in, w_in, conv_w)


def _mm_norm_res_kernel(a_ref, w_ref, res_ref, g1_ref, *rest, n_sub, has_next):
    if has_next:
        g2_ref, x_out, h_out = rest
    else:
        x_out, = rest
    sub = a_ref.shape[0] // n_sub
    for s in range(n_sub):
        rows = pl.ds(s * sub, sub)
        xn = res_ref[rows, :] + _rms(_dot(a_ref[rows, :], w_ref[...]), g1_ref[...])
        x_out[rows, :] = xn
        if has_next:
            h_out[rows, :] = _rms(xn, g2_ref[...]).astype(h_out.dtype)


def mm_norm_res(a, w, layer, res, g1, g2=None, tm=512, n_sub=4):
    m, kdim = a.shape
    d = w.shape[2]
    has_next = g2 is not None
    row = pl.BlockSpec((tm, d), lambda i: (i, 0))
    vec = pl.BlockSpec((1, d), lambda i: (0, 0))
    in_specs = [pl.BlockSpec((tm, kdim), lambda i: (i, 0)),
                pl.BlockSpec((None, kdim, d), lambda i: (layer, 0, 0), pipeline_mode=pl.Buffered(1)), row, vec]
    args = [a, w, res, g1.reshape(1, d)]
    out_specs, out_shape = [row], [jax.ShapeDtypeStruct((m, d), F32)]
    if has_next:
        in_specs.append(vec)
        args.append(g2.reshape(1, d))
        out_specs.append(row)
        out_shape.append(jax.ShapeDtypeStruct((m, d), BF16))
    out = pl.pallas_call(
        functools.partial(_mm_norm_res_kernel, n_sub=n_sub, has_next=has_next),
        grid=(m // tm,),
        in_specs=in_specs,
        out_specs=out_specs,
        out_shape=out_shape,
        compiler_params=_params(1),
        name="mm_norm_res",
    )(*args)
    return tuple(out) if has_next else (out[0], None)


def _ffn_up_kernel(h_ref, wg_ref, wu_ref, cg_ref, cu_ref, bg_ref, bu_ref, o_ref, wg_bf, wu_bf, carry_g, carry_u,
                   *, tiles_per_seq, tail_frac):
    _cast_weights_once([(wg_ref, wg_bf), (wu_ref, wu_bf)])
    first = pl.program_id(1) % tiles_per_seq == 0
    cg = jnp.where(first, 0.0, carry_g[...])
    cu = jnp.where(first, 0.0, carry_u[...])
    tm = h_ref.shape[0]
    cut = tm - tm // tail_frac
    pg = _dot(h_ref[...], wg_bf[...])
    pu_a = _dot(h_ref[0:cut, :], wu_bf[...])
    pu_b = _dot(h_ref[cut:tm, :], wu_bf[...])
    gate = _causal_conv3(pg, cg, cg_ref[...]) + bg_ref[...]
    sg = gate * _sigmoid(gate)
    up_a = _causal_conv3(pu_a, cu, cu_ref[...]) + bu_ref[...]
    o_ref[0:cut, :] = (sg[0:cut, :] * up_a).astype(o_ref.dtype)
    up_b = _causal_conv3(pu_b, pu_a[cut - 8:, :], cu_ref[...]) + bu_ref[...]
    o_ref[cut:tm, :] = (sg[cut:tm, :] * up_b).astype(o_ref.dtype)
    carry_g[...] = pg[tm - 8:, :]
    carry_u[...] = pu_b[tm - cut - 8:, :]


def ffn_up(h, w_up, layer, conv_w, conv_b, seq, tm=1024, tn=512, tail_frac=4):
    m, d = h.shape
    f = w_up.shape[2] // 2
    nj = f // tn
    conv_b = conv_b.reshape(1, 2 * f)
    wspec = lambda off: pl.BlockSpec((None, d, tn), lambda j, i: (layer, 0, off * nj + j))
    return pl.pallas_call(
        functools.partial(_ffn_up_kernel, tiles_per_seq=seq // tm, tail_frac=tail_frac),
        grid=(nj, m // tm),
        in_specs=[
            pl.BlockSpec((tm, d), lambda j, i: (i, 0)),
            wspec(0), wspec(1),
            pl.BlockSpec((3, tn), lambda j, i: (0, j)),
            pl.BlockSpec((3, tn), lambda j, i: (0, nj + j)),
            pl.BlockSpec((1, tn), lambda j, i: (0, j)),
            pl.BlockSpec((1, tn), lambda j, i: (0, nj + j)),
        ],
        out_specs=pl.BlockSpec((tm, tn), lambda j, i: (i, j)),
        out_shape=jax.ShapeDtypeStruct((m, f), BF16),
        scratch_shapes=[pltpu.VMEM((d, tn), BF16)] * 2 + [pltpu.VMEM((8, tn), F32)] * 2,
        compiler_params=_params(2),
        name="ffn_up",
    )(h, w_up, w_up, conv_w, conv_w, conv_b, conv_b)


def _token_shift_delta(h_ref, halo_ref, tiles_per_seq, axis):
    h = h_ref[...].astype(F32)
    prev = halo_ref[BF16_SUBLANES - 1:BF16_SUBLANES, :].astype(F32)
    prev = jnp.where(pl.program_id(axis) % tiles_per_seq == 0, 0.0, prev)
    return h, _shift_rows(h, prev) - h


def _rw_lora_kernel(h_ref, halo_ref, mu_ref, w1_ref, a1_ref, g1_ref, tw_ref, ta_ref, tg_ref, *, tiles_per_seq):
    h, xx = _token_shift_delta(h_ref, halo_ref, tiles_per_seq, 0)
    xw = (h + xx * mu_ref[1:2, :]).astype(BF16)
    xa = (h + xx * mu_ref[4:5, :]).astype(BF16)
    xg = (h + xx * mu_ref[5:6, :]).astype(BF16)
    tw_ref[...] = jnp.tanh(_dot(xw, w1_ref[...])).astype(tw_ref.dtype)
    ta_ref[...] = _dot(xa, a1_ref[...]).astype(ta_ref.dtype)
    tg_ref[...] = _sigmoid(_dot(xg, g1_ref[...])).astype(tg_ref.dtype)


def _halo_spec(tm, d, grid_axis):
    step = tm // BF16_SUBLANES
    if grid_axis == 0:
        return pl.BlockSpec((BF16_SUBLANES, d), lambda i: (jnp.maximum(i * step - 1, 0), 0))
    return pl.BlockSpec((BF16_SUBLANES, d), lambda j, i: (jnp.maximum(i * step - 1, 0), 0))


def rwkv_lora(h, mu, w1, a1, g1, seq, tm=512):
    m, d = h.shape
    full = lambda arr: pl.BlockSpec(arr.shape, lambda i: (0, 0))
    outs = [w1.shape[1], a1.shape[1], g1.shape[1]]
    return pl.pallas_call(
        functools.partial(_rw_lora_kernel, tiles_per_seq=seq // tm),
        grid=(m // tm,),
        in_specs=[pl.BlockSpec((tm, d), lambda i: (i, 0)), _halo_spec(tm, d, 0), full(mu), full(w1), full(a1), full(g1)],
        out_specs=[pl.BlockSpec((tm, n), lambda i: (i, 0)) for n in outs],
        out_shape=[jax.ShapeDtypeStruct((m, n), BF16) for n in outs],
        compiler_params=_params(1),
        name="rwkv_lora",
    )(h, h, mu, w1, a1, g1)


def _rw_proj_kernel(h_ref, halo_ref, mu_ref, wr_ref, wk_ref, wv_ref, tw_ref, ta_ref, tg_ref,
                    w2_ref, a2_ref, g2_ref, w0_ref, a0_ref, kk_ref, ka_ref, bd_ref,
                    r_out, ld_out, k_out, v_out, av_out, bv_out, g_out, *, tiles_per_seq, n_sub):
    h_all, xx_all = _token_shift_delta(h_ref, halo_ref, tiles_per_seq, 1)
    sub = h_ref.shape[0] // n_sub
    for s in range(n_sub):
        rows = pl.ds(s * sub, sub)
        h = h_all[s * sub:(s + 1) * sub, :]
        xx = xx_all[s * sub:(s + 1) * sub, :]
        xr = (h + xx * mu_ref[0:1, :]).astype(BF16)
        xk = (h + xx * mu_ref[2:3, :]).astype(BF16)
        xv = (h + xx * mu_ref[3:4, :]).astype(BF16)
        r = _dot(xr, wr_ref[...])
        k = _dot(xk, wk_ref[...])
        v = _dot(xv, wv_ref[...])
        wl = w0_ref[...] + _dot(tw_ref[rows, :], w2_ref[...])
        z = -wl
        w_log = -(jnp.maximum(z, 0.0) + jnp.log(1.0 + jnp.exp(-jnp.abs(z)))) - 0.5
        ld = -jnp.exp(w_log)
        a = _sigmoid(a0_ref[...] + _dot(ta_ref[rows, :], a2_ref[...]))
        g = _dot(tg_ref[rows, :], g2_ref[...])
        kk = k * kk_ref[...]
        nrm = jnp.sqrt(_head_sum(kk * kk, bd_ref[...]))
        kk = kk / jnp.maximum(nrm, L2_EPS)
        k = k * (1.0 + (a - 1.0) * ka_ref[...])
        r_out[rows, :] = r.astype(r_out.dtype)
        ld_out[rows, :] = ld
        k_out[rows, :] = k.astype(k_out.dtype)
        v_out[rows, :] = v.astype(v_out.dtype)
        av_out[rows, :] = (-kk).astype(av_out.dtype)
        bv_out[rows, :] = (kk * a).astype(bv_out.dtype)
        g_out[rows, :] = g.astype(g_out.dtype)


def rwkv_proj(h, mu, wr, wk, wv, tw, ta, tg, w2, a2, g2, w0, a0, k_k, k_a, bd, seq, tm=512, tn=512, n_sub=2):
    m, d = h.shape
    colw = lambda arr: pl.BlockSpec((arr.shape[0], tn), lambda j, i: (0, j))
    rowt = lambda arr: pl.BlockSpec((tm, arr.shape[1]), lambda j, i: (i, 0))
    vecs = [w0.reshape(1, d), a0.reshape(1, d), k_k.reshape(1, d), k_a.reshape(1, d)]
    out_dtypes = [BF16, F32, BF16, BF16, BF16, BF16, BF16]
    return pl.pallas_call(
        functools.partial(_rw_proj_kernel, tiles_per_seq=seq // tm, n_sub=n_sub),
        grid=(d // tn, m // tm),
        in_specs=[pl.BlockSpec((tm, d), lambda j, i: (i, 0)), _halo_spec(tm, d, 1),
                  pl.BlockSpec(mu.shape, lambda j, i: (0, 0)),
                  colw(wr), colw(wk), colw(wv), rowt(tw), rowt(ta), rowt(tg), colw(w2), colw(a2), colw(g2)]
                 + [colw(x) for x in vecs] + [pl.BlockSpec(bd.shape, lambda j, i: (0, 0))],
        out_specs=[pl.BlockSpec((tm, tn), lambda j, i: (i, j)) for _ in out_dtypes],
        out_shape=[jax.ShapeDtypeStruct((m, d), dt) for dt in out_dtypes],
        compiler_params=_params(2),
        name="rwkv_proj",
    )(h, h, mu, wr, wk, wv, tw, ta, tg, w2, a2, g2, *vecs, bd)


def _stack_pair(x, blk):
    return jnp.where(blk, jnp.concatenate([x, x], axis=0), 0.0).astype(BF16)


def _rw_scan_kernel(r_ref, ld_ref, k_ref, v_ref, av_ref, bv_ref, g_ref, rk_ref, gng_ref, gnb_ref, bd_ref,
                    o_ref, s_ref, *, n_pairs):
    L = CHUNK

    @pl.when(pl.program_id(2) == 0)
    def _():
        s_ref[...] = jnp.zeros_like(s_ref)

    n2 = 2 * L
    row = lax.broadcasted_iota(jnp.int32, (n2, n2), 0)
    col = lax.broadcasted_iota(jnp.int32, (n2, n2), 1)
    same = (row // L) == (col // L)
    strict = same & (row > col)
    incl = same & (row >= col)
    eye = (row == col).astype(F32)
    tr = lax.broadcasted_iota(jnp.int32, (L, 4 * L), 0)
    tc = lax.broadcasted_iota(jnp.int32, (L, 4 * L), 1)
    tri = ((tr >= tc % L) & (tc < 3 * L)).astype(BF16)
    bd = bd_ref[...]

    pairs = range(n_pairs)
    cols = [slice(p * PAIR, (p + 1) * PAIR) for p in pairs]
    ld = ld_ref[...]
    hi, mid, lo = _split3(ld)
    cum = _dot(tri, jnp.concatenate([hi, mid, lo, jnp.zeros_like(lo)], axis=0))
    e_pos = jnp.exp(cum)
    e_neg = jnp.exp(-cum)
    r = r_ref[...].astype(F32)
    k = k_ref[...].astype(F32)
    v = v_ref[...].astype(F32)
    rt_w = r * e_pos
    kt_w = k * e_neg
    bt_w = bv_ref[...].astype(F32) * e_neg
    at_w = av_ref[...].astype(F32) * jnp.exp(cum - ld)
    rt = [_stack_pair(rt_w[:, c], same) for c in cols]
    kt = [_stack_pair(kt_w[:, c], same) for c in cols]
    bt = [_stack_pair(bt_w[:, c], same) for c in cols]
    at = [_stack_pair(at_w[:, c], same) for c in cols]
    vs = [_stack_pair(v[:, c], same) for c in cols]

    cat0 = lambda xs: jnp.concatenate(xs, axis=0)
    ar = [cat0([at[p], rt[p]]) for p in pairs]
    bk = [cat0([bt[p], kt[p]]) for p in pairs]
    big = [_dot_nt(ar[p], bk[p]) for p in pairs]
    a_ab = [jnp.where(strict, x[:n2, :n2], 0.0) for x in big]
    a_ak = [jnp.where(strict, x[:n2, n2:], 0.0).astype(BF16) for x in big]
    a_rb = [jnp.where(incl, x[n2:, :n2], 0.0).astype(BF16) for x in big]
    a_rk = [jnp.where(incl, x[n2:, n2:], 0.0).astype(BF16) for x in big]

    inv = [eye + a for a in a_ab]
    ab = [a.astype(BF16) for a in a_ab]
    pw = [_dot(x, x) for x in ab]
    for _ in range(L.bit_length() - 3):
        pwb = [x.astype(BF16) for x in pw]
        both = [_dot(cat0([inv[p].astype(BF16), pwb[p]]), pwb[p]) for p in pairs]
        inv = [inv[p] + both[p][:n2, :] for p in pairs]
        pw = [both[p][n2:, :] for p in pairs]
    invb = [(inv[p] + _dot(inv[p].astype(BF16), pw[p].astype(BF16))).astype(BF16) for p in pairs]

    s = [s_ref[p] for p in pairs]
    sb = [x.astype(BF16) for x in s]
    ars = [_dot_nt(ar[p], sb[p]) for p in pairs]
    akv = [_dot(cat0([a_ak[p], a_rk[p]]), vs[p]) for p in pairs]
    ub = [_dot(invb[p], (ars[p][:n2, :] + akv[p][:n2, :]).astype(BF16)).astype(BF16) for p in pairs]
    y2 = [ars[p][n2:, :] + akv[p][n2:, :] + _dot(a_rb[p], ub[p]) for p in pairs]
    w_last = e_pos[L - 1:L, :]
    for p in pairs:
        s_ref[p] = (s[p] + _dot_tn(cat0([ub[p], vs[p]]), bk[p])) * w_last[:, cols[p]]
    y = jnp.concatenate([x[:L, :] + x[L:, :] for x in y2], axis=1)

    mean = _head_sum(y, bd) * (1.0 / HEAD_SIZE)
    yc = y - mean
    var = _head_sum(yc * yc, bd) * (1.0 / HEAD_SIZE)
    yn = yc * lax.rsqrt(var + GN_EPS) * gng_ref[...] + gnb_ref[...]
    bonus = _head_sum(r * k * rk_ref[...], bd) * v
    o_ref[...] = ((yn + bonus) * g_ref[...].astype(F32)).astype(o_ref.dtype)


def rwkv_scan(r, ld, k, v, av, bv, g, r_k, gn_g, gn_b, bd, seq, n_pairs=16):
    m, d = r.shape
    n_pairs = min(n_pairs, d // PAIR)
    cw = n_pairs * PAIR
    nc = seq // CHUNK
    blk = pl.BlockSpec((CHUNK, cw), lambda b, j, c: (b * nc + c, j))
    vec = pl.BlockSpec((1, cw), lambda b, j, c: (0, j))
    return pl.pallas_call(
        functools.partial(_rw_scan_kernel, n_pairs=n_pairs),
        grid=(m // seq, d // cw, nc),
        in_specs=[blk] * 7 + [vec] * 3 + [pl.BlockSpec(bd.shape, lambda b, j, c: (0, 0))],
        out_specs=blk,
        out_shape=jax.ShapeDtypeStruct((m, d), BF16),
        scratch_shapes=[pltpu.VMEM((n_pairs, PAIR, PAIR), F32)],
        compiler_params=_params(3),
        name="rwkv_scan",
    )(r, ld, k, v, av, bv, g, r_k.reshape(1, d), gn_g.reshape(1, d), gn_b.reshape(1, d), bd)


def _pad_cols(w, n):
    return jnp.pad(w, ((0, 0), (0, n - w.shape[1])))


def _pad_rows(w, n):
    return jnp.pad(w, ((0, n - w.shape[0]), (0, 0)))


def kernel(x, norm_g, sc_w_in, sc_conv, sc_w_out, rw_mu, rw_wr, rw_wk, rw_wv, rw_wo, rw_w0, rw_w1, rw_w2,
           rw_a0, rw_a1, rw_a2, rw_g1, rw_g2, rw_kk, rw_ka, rw_rk, rw_gn_g, rw_gn_b,
           ffn_w_up, ffn_conv, ffn_conv_b, ffn_w_down):
    bsz, seq, d = x.shape
    depth = norm_g.shape[0]
    xf = x.reshape(bsz * seq, d)
    lane = jnp.arange(LANES)
    bd = (lane[:, None] // HEAD_SIZE == lane[None, :] // HEAD_SIZE).astype(BF16)
    bf = lambda w: w.astype(BF16)
    n_conv, n_rwkv, f = sc_w_out.shape[0], rw_wr.shape[0], ffn_w_down.shape[1]
    square = cast_bf16([sc_w_out[j] for j in range(n_conv)]
                       + [w[j] for j in range(n_rwkv) for w in (rw_wr, rw_wk, rw_wv, rw_wo)])
    w_conv_out = [w[None] for w in square[:n_conv]]
    w_rwkv = [square[n_conv + 4 * j:n_conv + 4 * j + 4] for j in range(n_rwkv)]
    w_down = cast_bf16([ffn_w_down.reshape(depth * f, d)])[0].reshape(depth, f, d)

    h = norm_cast(xf, norm_g[0, 0])
    for i in range(depth):
        j = i // 2
        if i % 2 == 0:
            mix = shortconv_in(h, sc_w_in, j, sc_conv[j], seq)
            w_mix_out = w_conv_out[j]
        else:
            wr, wk, wv, wo = w_rwkv[j]
            lora = LANES * pl.cdiv(rw_w1.shape[2], LANES)
            tw, ta, tg = rwkv_lora(h, rw_mu[j], bf(_pad_cols(rw_w1[j], lora)), bf(_pad_cols(rw_a1[j], lora)),
                                   bf(rw_g1[j]), seq)
            parts = rwkv_proj(h, rw_mu[j], wr, wk, wv, tw, ta, tg,
                              bf(_pad_rows(rw_w2[j], lora)), bf(_pad_rows(rw_a2[j], lora)), bf(rw_g2[j]),
                              rw_w0[j], rw_a0[j], rw_kk[j], rw_ka[j], bd, seq)
            mix = rwkv_scan(*parts, rw_rk[j], rw_gn_g[j], rw_gn_b[j], bd, seq)
            w_mix_out = wo[None]
        xf, h = mm_norm_res(mix, w_mix_out, 0, xf, norm_g[i, 1], norm_g[i, 2], tm=512, n_sub=4)
        act = ffn_up(h, ffn_w_up, i, ffn_conv[i], ffn_conv_b[i], seq)
        g_next = norm_g[i + 1, 0] if i + 1 < depth else None
        xf, h = mm_norm_res(act, w_down, i, xf, norm_g[i, 3], g_next, tm=256, n_sub=2)
    return xf.reshape(bsz, seq, d)
```

```python
import functools
import math

import jax
import jax.numpy as jnp
from jax import lax
from jax.experimental import pallas as pl
from jax.experimental.pallas import tpu as pltpu

F32 = jnp.float32
BF16 = jnp.bfloat16

HEAD_SIZE = 64
RMS_EPS = 1e-6
GN_EPS = 64e-5
L2_EPS = 1e-12
DECAY_SCALE = math.exp(-0.5)

LANES = 128
BF16_SUBLANES = 16
VMEM_LIMIT_BYTES = 56 * 1024 * 1024
CAST_STEP_BYTES = 4 * 1024 * 1024
CHUNK = 64
PAIR = 2 * HEAD_SIZE


def _params(n_axes):
    return pltpu.CompilerParams(dimension_semantics=("arbitrary",) * n_axes,
                                vmem_limit_bytes=VMEM_LIMIT_BYTES)


def _dot(a, b):
    return jnp.dot(a, b, preferred_element_type=F32)


def _dot_nt(a, b):
    return lax.dot_general(a, b, (((1,), (1,)), ((), ())), preferred_element_type=F32)


def _dot_tn(a, b):
    return lax.dot_general(a, b, (((0,), (0,)), ((), ())), preferred_element_type=F32)


def _split2(x):
    hi = x.astype(BF16)
    lo = (x - hi.astype(F32)).astype(BF16)
    return hi, lo


def _split3(x):
    hi = x.astype(BF16)
    r1 = x - hi.astype(F32)
    mid = r1.astype(BF16)
    lo = (r1 - mid.astype(F32)).astype(BF16)
    return hi, mid, lo


def _rms(y, g):
    return y * lax.rsqrt(jnp.mean(y * y, axis=-1, keepdims=True) + RMS_EPS) * g


def _sigmoid(x):
    return 1.0 / (1.0 + jnp.exp(-x))


def _head_sum(x, bd):
    rows, n = x.shape[0], x.shape[1] // LANES
    if n > 1:
        x = jnp.concatenate([x[:, s * LANES:(s + 1) * LANES] for s in range(n)], axis=0)
    hi, lo = _split2(x)
    out = _dot(jnp.concatenate([hi, lo], axis=1), jnp.concatenate([bd, bd], axis=0))
    if n > 1:
        out = jnp.concatenate([out[s * rows:(s + 1) * rows, :] for s in range(n)], axis=1)
    return out


def _shift_rows(p, prev1, prev2=None):
    row = lax.broadcasted_iota(jnp.int32, p.shape, 0)
    p1 = jnp.where(row == 0, prev1, pltpu.roll(p, 1, 0))
    if prev2 is None:
        return p1
    p2 = jnp.where(row == 0, prev2, jnp.where(row == 1, prev1, pltpu.roll(p, 2, 0)))
    return p1, p2


def _causal_conv3(p, carry, w):
    p1, p2 = _shift_rows(p, carry[7:8, :], carry[6:7, :])
    return w[0:1, :] * p2 + w[1:2, :] * p1 + w[2:3, :] * p


def _norm_cast_kernel(x_ref, g_ref, o_ref):
    o_ref[...] = _rms(x_ref[...], g_ref[...]).astype(o_ref.dtype)


def norm_cast(x, g, tm=512):
    m, d = x.shape
    return pl.pallas_call(
        _norm_cast_kernel,
        grid=(m // tm,),
        in_specs=[pl.BlockSpec((tm, d), lambda i: (i, 0)), pl.BlockSpec((1, d), lambda i: (0, 0))],
        out_specs=pl.BlockSpec((tm, d), lambda i: (i, 0)),
        out_shape=jax.ShapeDtypeStruct((m, d), BF16),
        compiler_params=_params(1),
        name="norm_cast",
    )(x, g.reshape(1, d))


def _cast_kernel(*refs):
    n = len(refs) // 2
    for src_ref, dst_ref in zip(refs[:n], refs[n:]):
        dst_ref[...] = src_ref[...].astype(dst_ref.dtype)


def cast_bf16(ws):
    rows, cols = ws[0].shape
    tm = CAST_STEP_BYTES // (len(ws) * cols * 4)
    tm = min(rows, 1 << (tm.bit_length() - 1))
    spec = pl.BlockSpec((tm, cols), lambda i: (i, 0))
    return pl.pallas_call(
        _cast_kernel,
        grid=(rows // tm,),
        in_specs=[spec] * len(ws),
        out_specs=[spec] * len(ws),
        out_shape=[jax.ShapeDtypeStruct((rows, cols), BF16)] * len(ws),
        compiler_params=_params(1),
        name="cast_bf16",
    )(*ws)


def _cast_weights_once(pairs):
    @pl.when(pl.program_id(1) == 0)
    def _():
        for src_ref, dst_ref in pairs:
            dst_ref[...] = src_ref[...].astype(dst_ref.dtype)


def _sc_in_kernel(h_ref, wb_ref, wc_ref, wh_ref, cw_ref, o_ref, wb_bf, wc_bf, wh_bf, carry_ref, *, tiles_per_seq, n_sub):
    _cast_weights_once([(wb_ref, wb_bf), (wc_ref, wc_bf), (wh_ref, wh_bf)])
    carry = jnp.where(pl.program_id(1) % tiles_per_seq == 0, 0.0, carry_ref[...])
    sub = h_ref.shape[0] // n_sub
    cw = cw_ref[...]
    for s in range(n_sub):
        rows = pl.ds(s * sub, sub)
        h = h_ref[rows, :]
        p = _dot(h, wc_bf[...]) * _dot(h, wh_bf[...])
        y = _causal_conv3(p, carry, cw)
        carry = p[sub - 8:, :]
        o_ref[rows, :] = (_dot(h, wb_bf[...]) * y).astype(o_ref.dtype)
    carry_ref[...] = carry


def shortconv_in(h, w_in, layer, conv_w, seq, tm=1024, tn=512, n_sub=1):
    m, d = h.shape
    nj = d // tn
    wspec = lambda off: pl.BlockSpec((None, d, tn), lambda j, i: (layer, 0, off * nj + j))
    return pl.pallas_call(
        functools.partial(_sc_in_kernel, tiles_per_seq=seq // tm, n_sub=n_sub),
        grid=(nj, m // tm),
        in_specs=[
            pl.BlockSpec((tm, d), lambda j, i: (i, 0)),
            wspec(0), wspec(1), wspec(2),
            pl.BlockSpec((3, tn), lambda j, i: (0, j)),
        ],
        out_specs=pl.BlockSpec((tm, tn), lambda j, i: (i, j)),
        out_shape=jax.ShapeDtypeStruct((m, d), BF16),
        scratch_shapes=[pltpu.VMEM((d, tn), BF16)] * 3 + [pltpu.VMEM((8, tn), F32)],
        compiler_params=_params(2),
        name="shortconv_in",
    )(h, w_in, w_in, w_in, conv_w)


def _mm_norm_res_kernel(a_ref, w_ref, res_ref, g1_ref, *rest, n_sub, has_next):
    if has_next:
        g2_ref, x_out, h_out = rest
    else:
        x_out, = rest
    sub = a_ref.shape[0] // n_sub
    for s in range(n_sub):
        rows = pl.ds(s * sub, sub)
        xn = res_ref[rows, :] + _rms(_dot(a_ref[rows, :], w_ref[...]), g1_ref[...])
        x_out[rows, :] = xn
        if has_next:
            h_out[rows, :] = _rms(xn, g2_ref[...]).astype(h_out.dtype)


def mm_norm_res(a, w, layer, res, g1, g2=None, tm=512, n_sub=4):
    m, kdim = a.shape
    d = w.shape[2]
    has_next = g2 is not None
    row = pl.BlockSpec((tm, d), lambda i: (i, 0))
    vec = pl.BlockSpec((1, d), lambda i: (0, 0))
    in_specs = [pl.BlockSpec((tm, kdim), lambda i: (i, 0)),
                pl.BlockSpec((None, kdim, d), lambda i: (layer, 0, 0), pipeline_mode=pl.Buffered(1)), row, vec]
    args = [a, w, res, g1.reshape(1, d)]
    out_specs, out_shape = [row], [jax.ShapeDtypeStruct((m, d), F32)]
    if has_next:
        in_specs.append(vec)
        args.append(g2.reshape(1, d))
        out_specs.append(row)
        out_shape.append(jax.ShapeDtypeStruct((m, d), BF16))
    out = pl.pallas_call(
        functools.partial(_mm_norm_res_kernel, n_sub=n_sub, has_next=has_next),
        grid=(m // tm,),
        in_specs=in_specs,
        out_specs=out_specs,
        out_shape=out_shape,
        compiler_params=_params(1),
        name="mm_norm_res",
    )(*args)
    return tuple(out) if has_next else (out[0], None)


def _ffn_up_kernel(h_ref, wg_ref, wu_ref, cg_ref, cu_ref, bg_ref, bu_ref, o_ref, wg_bf, wu_bf, carry_g, carry_u,
                   *, tiles_per_seq, tail_frac):
    _cast_weights_once([(wg_ref, wg_bf), (wu_ref, wu_bf)])
    first = pl.program_id(1) % tiles_per_seq == 0
    cg = jnp.where(first, 0.0, carry_g[...])
    cu = jnp.where(first, 0.0, carry_u[...])
    tm = h_ref.shape[0]
    cut = tm - tm // tail_frac
    pg = _dot(h_ref[...], wg_bf[...])
    pu_a = _dot(h_ref[0:cut, :], wu_bf[...])
    pu_b = _dot(h_ref[cut:tm, :], wu_bf[...])
    gate = _causal_conv3(pg, cg, cg_ref[...]) + bg_ref[...]
    sg = gate * _sigmoid(gate)
    up_a = _causal_conv3(pu_a, cu, cu_ref[...]) + bu_ref[...]
    o_ref[0:cut, :] = (sg[0:cut, :] * up_a).astype(o_ref.dtype)
    up_b = _causal_conv3(pu_b, pu_a[cut - 8:, :], cu_ref[...]) + bu_ref[...]
    o_ref[cut:tm, :] = (sg[cut:tm, :] * up_b).astype(o_ref.dtype)
    carry_g[...] = pg[tm - 8:, :]
    carry_u[...] = pu_b[tm - cut - 8:, :]


def ffn_up(h, w_up, layer, conv_w, conv_b, seq, tm=1024, tn=512, tail_frac=4):
    m, d = h.shape
    f = w_up.shape[2] // 2
    nj = f // tn
    conv_b = conv_b.reshape(1, 2 * f)
    wspec = lambda off: pl.BlockSpec((None, d, tn), lambda j, i: (layer, 0, off * nj + j))
    return pl.pallas_call(
        functools.partial(_ffn_up_kernel, tiles_per_seq=seq // tm, tail_frac=tail_frac),
        grid=(nj, m // tm),
        in_specs=[
            pl.BlockSpec((tm, d), lambda j, i: (i, 0)),
            wspec(0), wspec(1),
            pl.BlockSpec((3, tn), lambda j, i: (0, j)),
            pl.BlockSpec((3, tn), lambda j, i: (0, nj + j)),
            pl.BlockSpec((1, tn), lambda j, i: (0, j)),
            pl.BlockSpec((1, tn), lambda j, i: (0, nj + j)),
        ],
        out_specs=pl.BlockSpec((tm, tn), lambda j, i: (i, j)),
        out_shape=jax.ShapeDtypeStruct((m, f), BF16),
        scratch_shapes=[pltpu.VMEM((d, tn), BF16)] * 2 + [pltpu.VMEM((8, tn), F32)] * 2,
        compiler_params=_params(2),
        name="ffn_up",
    )(h, w_up, w_up, conv_w, conv_w, conv_b, conv_b)


def _token_shift_delta(h_ref, halo_ref, tiles_per_seq, axis):
    h = h_ref[...].astype(F32)
    prev = halo_ref[BF16_SUBLANES - 1:BF16_SUBLANES, :].astype(F32)
    prev = jnp.where(pl.program_id(axis) % tiles_per_seq == 0, 0.0, prev)
    return h, _shift_rows(h, prev) - h


def _rw_lora_kernel(h_ref, halo_ref, mu_ref, w1_ref, a1_ref, g1_ref, tw_ref, ta_ref, tg_ref, *, tiles_per_seq):
    h, xx = _token_shift_delta(h_ref, halo_ref, tiles_per_seq, 0)
    xw = (h + xx * mu_ref[1:2, :]).astype(BF16)
    xa = (h + xx * mu_ref[4:5, :]).astype(BF16)
    xg = (h + xx * mu_ref[5:6, :]).astype(BF16)
    tw_ref[...] = jnp.tanh(_dot(xw, w1_ref[...])).astype(tw_ref.dtype)
    ta_ref[...] = _dot(xa, a1_ref[...]).astype(ta_ref.dtype)
    tg_ref[...] = _sigmoid(_dot(xg, g1_ref[...])).astype(tg_ref.dtype)


def _halo_spec(tm, d, grid_axis):
    step = tm // BF16_SUBLANES
    if grid_axis == 0:
        return pl.BlockSpec((BF16_SUBLANES, d), lambda i: (jnp.maximum(i * step - 1, 0), 0))
    return pl.BlockSpec((BF16_SUBLANES, d), lambda j, i: (jnp.maximum(i * step - 1, 0), 0))


def rwkv_lora(h, mu, w1, a1, g1, seq, tm=512):
    m, d = h.shape
    full = lambda arr: pl.BlockSpec(arr.shape, lambda i: (0, 0))
    outs = [w1.shape[1], a1.shape[1], g1.shape[1]]
    return pl.pallas_call(
        functools.partial(_rw_lora_kernel, tiles_per_seq=seq // tm),
        grid=(m // tm,),
        in_specs=[pl.BlockSpec((tm, d), lambda i: (i, 0)), _halo_spec(tm, d, 0), full(mu), full(w1), full(a1), full(g1)],
        out_specs=[pl.BlockSpec((tm, n), lambda i: (i, 0)) for n in outs],
        out_shape=[jax.ShapeDtypeStruct((m, n), BF16) for n in outs],
        compiler_params=_params(1),
        name="rwkv_lora",
    )(h, h, mu, w1, a1, g1)


def _rw_proj_kernel(h_ref, halo_ref, mu_ref, wr_ref, wk_ref, wv_ref, tw_ref, ta_ref, tg_ref,
                    w2_ref, a2_ref, g2_ref, w0_ref, a0_ref, kk_ref, ka_ref, bd_ref,
                    r_out, ld_out, k_out, v_out, av_out, bv_out, g_out, *, tiles_per_seq, n_sub):
    h_all, xx_all = _token_shift_delta(h_ref, halo_ref, tiles_per_seq, 1)
    sub = h_ref.shape[0] // n_sub
    for s in range(n_sub):
        rows = pl.ds(s * sub, sub)
        h = h_all[s * sub:(s + 1) * sub, :]
        xx = xx_all[s * sub:(s + 1) * sub, :]
        xr = (h + xx * mu_ref[0:1, :]).astype(BF16)
        xk = (h + xx * mu_ref[2:3, :]).astype(BF16)
        xv = (h + xx * mu_ref[3:4, :]).astype(BF16)
        r = _dot(xr, wr_ref[...])
        k = _dot(xk, wk_ref[...])
        v = _dot(xv, wv_ref[...])
        wl = w0_ref[...] + _dot(tw_ref[rows, :], w2_ref[...])
        ld = -DECAY_SCALE * _sigmoid(wl)
        a = _sigmoid(a0_ref[...] + _dot(ta_ref[rows, :], a2_ref[...]))
        g = _dot(tg_ref[rows, :], g2_ref[...])
        kk = k * kk_ref[...]
        nrm = jnp.sqrt(_head_sum(kk * kk, bd_ref[...]))
        kk = kk / jnp.maximum(nrm, L2_EPS)
        k = k * (1.0 + (a - 1.0) * ka_ref[...])
        r_out[rows, :] = r.astype(r_out.dtype)
        ld_out[rows, :] = ld
        k_out[rows, :] = k.astype(k_out.dtype)
        v_out[rows, :] = v.astype(v_out.dtype)
        av_out[rows, :] = (-kk).astype(av_out.dtype)
        bv_out[rows, :] = (kk * a).astype(bv_out.dtype)
        g_out[rows, :] = g.astype(g_out.dtype)


def rwkv_proj(h, mu, wr, wk, wv, tw, ta, tg, w2, a2, g2, w0, a0, k_k, k_a, bd, seq, tm=256, n_sub=1):
    m, d = h.shape
    tn = d
    colw = lambda arr: pl.BlockSpec((arr.shape[0], tn), lambda j, i: (0, j), pipeline_mode=pl.Buffered(1))
    rowt = lambda arr: pl.BlockSpec((tm, arr.shape[1]), lambda j, i: (i, 0))
    vecs = [w0.reshape(1, d), a0.reshape(1, d), k_k.reshape(1, d), k_a.reshape(1, d)]
    out_dtypes = [BF16, F32, BF16, BF16, BF16, BF16, BF16]
    return pl.pallas_call(
        functools.partial(_rw_proj_kernel, tiles_per_seq=seq // tm, n_sub=n_sub),
        grid=(d // tn, m // tm),
        in_specs=[pl.BlockSpec((tm, d), lambda j, i: (i, 0)), _halo_spec(tm, d, 1),
                  pl.BlockSpec(mu.shape, lambda j, i: (0, 0)),
                  colw(wr), colw(wk), colw(wv), rowt(tw), rowt(ta), rowt(tg), colw(w2), colw(a2), colw(g2)]
                 + [colw(x) for x in vecs] + [pl.BlockSpec(bd.shape, lambda j, i: (0, 0))],
        out_specs=[pl.BlockSpec((tm, tn), lambda j, i: (i, j)) for _ in out_dtypes],
        out_shape=[jax.ShapeDtypeStruct((m, d), dt) for dt in out_dtypes],
        compiler_params=_params(2),
        name="rwkv_proj",
    )(h, h, mu, wr, wk, wv, tw, ta, tg, w2, a2, g2, *vecs, bd)


def _stack_pair(x, blk):
    return jnp.where(blk, jnp.concatenate([x, x], axis=0), 0.0).astype(BF16)


def _rw_scan_kernel(r_ref, ld_ref, k_ref, v_ref, av_ref, bv_ref, g_ref, rk_ref, gng_ref, gnb_ref, bd_ref,
                    o_ref, s_ref, *, n_pairs):
    L = CHUNK

    @pl.when(pl.program_id(2) == 0)
    def _():
        s_ref[...] = jnp.zeros_like(s_ref)

    n2 = 2 * L
    row = lax.broadcasted_iota(jnp.int32, (n2, n2), 0)
    col = lax.broadcasted_iota(jnp.int32, (n2, n2), 1)
    same = (row // L) == (col // L)
    strict = same & (row > col)
    incl = same & (row >= col)
    eye = (row == col).astype(F32)
    tr = lax.broadcasted_iota(jnp.int32, (L, 4 * L), 0)
    tc = lax.broadcasted_iota(jnp.int32, (L, 4 * L), 1)
    tri = ((tr >= tc % L) & (tc < 3 * L)).astype(BF16)
    bd = bd_ref[...]

    pairs = range(n_pairs)
    cols = [slice(p * PAIR, (p + 1) * PAIR) for p in pairs]
    cat0 = lambda xs: jnp.concatenate(xs, axis=0)

    def state_free(rows):
        ld = ld_ref[rows, :]
        hi, mid, lo = _split3(ld)
        cum = _dot(tri, jnp.concatenate([hi, mid, lo, jnp.zeros_like(lo)], axis=0))
        e_pos = jnp.exp(cum)
        e_neg = jnp.exp(-cum)
        r = r_ref[rows, :].astype(F32)
        k = k_ref[rows, :].astype(F32)
        v = v_ref[rows, :].astype(F32)
        rt_w = r * e_pos
        kt_w = k * e_neg
        bt_w = bv_ref[rows, :].astype(F32) * e_neg
        at_w = av_ref[rows, :].astype(F32) * jnp.exp(cum - ld)
        rt = [_stack_pair(rt_w[:, c], same) for c in cols]
        kt = [_stack_pair(kt_w[:, c], same) for c in cols]
        bt = [_stack_pair(bt_w[:, c], same) for c in cols]
        at = [_stack_pair(at_w[:, c], same) for c in cols]
        vs = [_stack_pair(v[:, c], same) for c in cols]

        ar = [cat0([at[p], rt[p]]) for p in pairs]
        bk = [cat0([bt[p], kt[p]]) for p in pairs]
        big = [_dot_nt(ar[p], bk[p]) for p in pairs]
        a_ab = [jnp.where(strict, x[:n2, :n2], 0.0) for x in big]
        a_ak = [jnp.where(strict, x[:n2, n2:], 0.0).astype(BF16) for x in big]
        a_rb = [jnp.where(incl, x[n2:, :n2], 0.0).astype(BF16) for x in big]
        a_rk = [jnp.where(incl, x[n2:, n2:], 0.0).astype(BF16) for x in big]

        inv = [eye + a for a in a_ab]
        ab = [a.astype(BF16) for a in a_ab]
        pw = [_dot(x, x) for x in ab]
        for _ in range(L.bit_length() - 3):
            pwb = [x.astype(BF16) for x in pw]
            both = [_dot(cat0([inv[p].astype(BF16), pwb[p]]), pwb[p]) for p in pairs]
            inv = [inv[p] + both[p][:n2, :] for p in pairs]
            pw = [both[p][n2:, :] for p in pairs]
        invb = [(inv[p] + _dot(inv[p].astype(BF16), pw[p].astype(BF16))).astype(BF16) for p in pairs]
        akv = [_dot(cat0([a_ak[p], a_rk[p]]), vs[p]) for p in pairs]
        bonus = _head_sum(r * k * rk_ref[...], bd) * v
        return dict(ar=ar, bk=bk, vs=vs, a_rb=a_rb, invb=invb, akv=akv, bonus=bonus, w_last=e_pos[L - 1:L, :])

    def advance(c, s):
        sb = [x.astype(BF16) for x in s]
        ars = [_dot_nt(c["ar"][p], sb[p]) for p in pairs]
        ub = [_dot(c["invb"][p], (ars[p][:n2, :] + c["akv"][p][:n2, :]).astype(BF16)).astype(BF16) for p in pairs]
        y2 = [ars[p][n2:, :] + c["akv"][p][n2:, :] + _dot(c["a_rb"][p], ub[p]) for p in pairs]
        s = [(s[p] + _dot_tn(cat0([ub[p], c["vs"][p]]), c["bk"][p])) * c["w_last"][:, cols[p]] for p in pairs]
        return jnp.concatenate([x[:L, :] + x[L:, :] for x in y2], axis=1), s

    def finish(rows, c, y):
        mean = _head_sum(y, bd) * (1.0 / HEAD_SIZE)
        yc = y - mean
        var = _head_sum(yc * yc, bd) * (1.0 / HEAD_SIZE)
        yn = yc * lax.rsqrt(var + GN_EPS) * gng_ref[...] + gnb_ref[...]
        o_ref[rows, :] = ((yn + c["bonus"]) * g_ref[rows, :].astype(F32)).astype(o_ref.dtype)

    chunk_rows = [pl.ds(i * L, L) for i in range(r_ref.shape[0] // L)]
    free = [state_free(rows) for rows in chunk_rows]
    s = [s_ref[p] for p in pairs]
    for rows, c in zip(chunk_rows, free):
        y, s = advance(c, s)
        finish(rows, c, y)
    for p in pairs:
        s_ref[p] = s[p]


def rwkv_scan(r, ld, k, v, av, bv, g, r_k, gn_g, gn_b, bd, seq, n_pairs=16, chunks_per_step=2):
    m, d = r.shape
    n_pairs = min(n_pairs, d // PAIR)
    cw = n_pairs * PAIR
    nc = seq // (CHUNK * chunks_per_step)
    blk = pl.BlockSpec((CHUNK * chunks_per_step, cw), lambda b, j, c: (b * nc + c, j))
    vec = pl.BlockSpec((1, cw), lambda b, j, c: (0, j))
    return pl.pallas_call(
        functools.partial(_rw_scan_kernel, n_pairs=n_pairs),
        grid=(m // seq, d // cw, nc),
        in_specs=[blk] * 7 + [vec] * 3 + [pl.BlockSpec(bd.shape, lambda b, j, c: (0, 0))],
        out_specs=blk,
        out_shape=jax.ShapeDtypeStruct((m, d), BF16),
        scratch_shapes=[pltpu.VMEM((n_pairs, PAIR, PAIR), F32)],
        compiler_params=_params(3),
        name="rwkv_scan",
    )(r, ld, k, v, av, bv, g, r_k.reshape(1, d), gn_g.reshape(1, d), gn_b.reshape(1, d), bd)


def _pad_cols(w, n):
    return jnp.pad(w, ((0, 0), (0, n - w.shape[1])))


def _pad_rows(w, n):
    return jnp.pad(w, ((0, n - w.shape[0]), (0, 0)))


def kernel(x, norm_g, sc_w_in, sc_conv, sc_w_out, rw_mu, rw_wr, rw_wk, rw_wv, rw_wo, rw_w0, rw_w1, rw_w2,
           rw_a0, rw_a1, rw_a2, rw_g1, rw_g2, rw_kk, rw_ka, rw_rk, rw_gn_g, rw_gn_b,
           ffn_w_up, ffn_conv, ffn_conv_b, ffn_w_down):
    bsz, seq, d = x.shape
    depth = norm_g.shape[0]
    xf = x.reshape(bsz * seq, d)
    lane = jnp.arange(LANES)
    bd = (lane[:, None] // HEAD_SIZE == lane[None, :] // HEAD_SIZE).astype(BF16)
    bf = lambda w: w.astype(BF16)
    n_conv, n_rwkv, f = sc_w_out.shape[0], rw_wr.shape[0], ffn_w_down.shape[1]
    square = cast_bf16([sc_w_out[j] for j in range(n_conv)]
                       + [w[j] for j in range(n_rwkv) for w in (rw_wr, rw_wk, rw_wv, rw_wo)])
    w_conv_out = [w[None] for w in square[:n_conv]]
    w_rwkv = [square[n_conv + 4 * j:n_conv + 4 * j + 4] for j in range(n_rwkv)]
    w_down = cast_bf16([ffn_w_down.reshape(depth * f, d)])[0].reshape(depth, f, d)

    h = norm_cast(xf, norm_g[0, 0])
    for i in range(depth):
        j = i // 2
        if i % 2 == 0:
            mix = shortconv_in(h, sc_w_in, j, sc_conv[j], seq)
            w_mix_out = w_conv_out[j]
        else:
            wr, wk, wv, wo = w_rwkv[j]
            lora = LANES * pl.cdiv(rw_w1.shape[2], LANES)
            tw, ta, tg = rwkv_lora(h, rw_mu[j], bf(_pad_cols(rw_w1[j], lora)), bf(_pad_cols(rw_a1[j], lora)),
                                   bf(rw_g1[j]), seq)
            parts = rwkv_proj(h, rw_mu[j], wr, wk, wv, tw, ta, tg,
                              bf(_pad_rows(rw_w2[j], lora)), bf(_pad_rows(rw_a2[j], lora)), bf(rw_g2[j]),
                              rw_w0[j], rw_a0[j], rw_kk[j], rw_ka[j], bd, seq)
            mix = rwkv_scan(*parts, rw_rk[j], rw_gn_g[j], rw_gn_b[j], bd, seq)
            w_mix_out = wo[None]
        xf, h = mm_norm_res(mix, w_mix_out, 0, xf, norm_g[i, 1], norm_g[i, 2], tm=512, n_sub=4)
        act = ffn_up(h, ffn_w_up, i, ffn_conv[i], ffn_conv_b[i], seq)
        g_next = norm_g[i + 1, 0] if i + 1 < depth else None
        xf, h = mm_norm_res(act, w_down, i, xf, norm_g[i, 3], g_next, tm=256, n_sub=2)
    return xf.reshape(bsz, seq, d)
```

```python
import functools
import math

import jax
import jax.numpy as jnp
from jax import lax
from jax.experimental import pallas as pl
from jax.experimental.pallas import tpu as pltpu

F32 = jnp.float32
BF16 = jnp.bfloat16

HEAD_SIZE = 64
RMS_EPS = 1e-6
GN_EPS = 64e-5
L2_EPS = 1e-12
DECAY_SCALE = math.exp(-0.5)

LANES = 128
BF16_SUBLANES = 16
VMEM_LIMIT_BYTES = 56 * 1024 * 1024
CHUNK = 64
PAIR = 2 * HEAD_SIZE


def _params(n_axes):
    return pltpu.CompilerParams(dimension_semantics=("arbitrary",) * n_axes,
                                vmem_limit_bytes=VMEM_LIMIT_BYTES)


def _dot(a, b):
    return jnp.dot(a, b, preferred_element_type=F32)


def _dot_nt(a, b):
    return lax.dot_general(a, b, (((1,), (1,)), ((), ())), preferred_element_type=F32)


def _dot_tn(a, b):
    return lax.dot_general(a, b, (((0,), (0,)), ((), ())), preferred_element_type=F32)


def _split2(x):
    hi = x.astype(BF16)
    lo = (x - hi.astype(F32)).astype(BF16)
    return hi, lo


def _split3(x):
    hi = x.astype(BF16)
    r1 = x - hi.astype(F32)
    mid = r1.astype(BF16)
    lo = (r1 - mid.astype(F32)).astype(BF16)
    return hi, mid, lo


def _rms(y, g):
    return y * lax.rsqrt(jnp.mean(y * y, axis=-1, keepdims=True) + RMS_EPS) * g


def _sigmoid(x):
    return 1.0 / (1.0 + jnp.exp(-x))


def _head_sum(x, bd):
    rows, n = x.shape[0], x.shape[1] // LANES
    if n > 1:
        x = jnp.concatenate([x[:, s * LANES:(s + 1) * LANES] for s in range(n)], axis=0)
    hi, lo = _split2(x)
    out = _dot(jnp.concatenate([hi, lo], axis=1), jnp.concatenate([bd, bd], axis=0))
    if n > 1:
        out = jnp.concatenate([out[s * rows:(s + 1) * rows, :] for s in range(n)], axis=1)
    return out


def _shift_rows(p, prev1, prev2=None):
    row = lax.broadcasted_iota(jnp.int32, p.shape, 0)
    p1 = jnp.where(row == 0, prev1, pltpu.roll(p, 1, 0))
    if prev2 is None:
        return p1
    p2 = jnp.where(row == 0, prev2, jnp.where(row == 1, prev1, pltpu.roll(p, 2, 0)))
    return p1, p2


def _causal_conv3(p, carry, w):
    p1, p2 = _shift_rows(p, carry[7:8, :], carry[6:7, :])
    return w[0:1, :] * p2 + w[1:2, :] * p1 + w[2:3, :] * p


def _norm_cast_kernel(x_ref, g_ref, o_ref):
    o_ref[...] = _rms(x_ref[...], g_ref[...]).astype(o_ref.dtype)


def norm_cast(x, g, tm=512):
    m, d = x.shape
    return pl.pallas_call(
        _norm_cast_kernel,
        grid=(m // tm,),
        in_specs=[pl.BlockSpec((tm, d), lambda i: (i, 0)), pl.BlockSpec((1, d), lambda i: (0, 0))],
        out_specs=pl.BlockSpec((tm, d), lambda i: (i, 0)),
        out_shape=jax.ShapeDtypeStruct((m, d), BF16),
        compiler_params=_params(1),
        name="norm_cast",
    )(x, g.reshape(1, d))


def _side_cast_specs(side, n_steps, n_inner):
    in_specs, out_specs, out_shapes = [], [], []
    for w, layer in side:
        _, rows, cols = w.shape
        blk = BF16_SUBLANES
        while rows % blk or rows // blk > n_steps:
            blk *= 2
        last = rows // blk - 1
        in_specs.append(pl.BlockSpec(
            (None, blk, cols), lambda j, i, layer=layer, last=last: (layer, jnp.minimum(j * n_inner + i, last), 0)))
        out_specs.append(pl.BlockSpec((blk, cols), lambda j, i, last=last: (jnp.minimum(j * n_inner + i, last), 0)))
        out_shapes.append(jax.ShapeDtypeStruct((rows, cols), BF16))
    return in_specs, out_specs, out_shapes


def _side_cast(src_refs, dst_refs):
    for src_ref, dst_ref in zip(src_refs, dst_refs):
        dst_ref[...] = src_ref[...].astype(dst_ref.dtype)


def _cast_weights_once(pairs):
    @pl.when(pl.program_id(1) == 0)
    def _():
        for src_ref, dst_ref in pairs:
            dst_ref[...] = src_ref[...].astype(dst_ref.dtype)


def _sc_in_kernel(*refs, tiles_per_seq, n_sub, n_side):
    h_ref, wb_ref, wc_ref, wh_ref, cw_ref = refs[:5]
    side_in, (o_ref, *side_out) = refs[5:5 + n_side], refs[5 + n_side:6 + 2 * n_side]
    wb_bf, wc_bf, wh_bf, carry_ref = refs[6 + 2 * n_side:]
    _side_cast(side_in, side_out)
    _cast_weights_once([(wb_ref, wb_bf), (wc_ref, wc_bf), (wh_ref, wh_bf)])
    carry = jnp.where(pl.program_id(1) % tiles_per_seq == 0, 0.0, carry_ref[...])
    sub = h_ref.shape[0] // n_sub
    cw = cw_ref[...]
    for s in range(n_sub):
        rows = pl.ds(s * sub, sub)
        h = h_ref[rows, :]
        p = _dot(h, wc_bf[...]) * _dot(h, wh_bf[...])
        y = _causal_conv3(p, carry, cw)
        carry = p[sub - 8:, :]
        o_ref[rows, :] = (_dot(h, wb_bf[...]) * y).astype(o_ref.dtype)
    carry_ref[...] = carry


def shortconv_in(h, w_in, layer, conv_w, seq, side=(), tm=1024, tn=512, n_sub=1):
    m, d = h.shape
    nj = d // tn
    side_in, side_out, side_shapes = _side_cast_specs(side, nj * (m // tm), m // tm)
    wspec = lambda off: pl.BlockSpec((None, d, tn), lambda j, i: (layer, 0, off * nj + j))
    return pl.pallas_call(
        functools.partial(_sc_in_kernel, tiles_per_seq=seq // tm, n_sub=n_sub, n_side=len(side)),
        grid=(nj, m // tm),
        in_specs=[
            pl.BlockSpec((tm, d), lambda j, i: (i, 0)),
            wspec(0), wspec(1), wspec(2),
            pl.BlockSpec((3, tn), lambda j, i: (0, j)),
        ] + side_in,
        out_specs=[pl.BlockSpec((tm, tn), lambda j, i: (i, j))] + side_out,
        out_shape=[jax.ShapeDtypeStruct((m, d), BF16)] + side_shapes,
        scratch_shapes=[pltpu.VMEM((d, tn), BF16)] * 3 + [pltpu.VMEM((8, tn), F32)],
        compiler_params=_params(2),
        name="shortconv_in",
    )(h, w_in, w_in, w_in, conv_w, *[w for w, _ in side])


def _mm_norm_res_kernel(a_ref, w_ref, res_ref, g1_ref, *rest, n_sub, has_next):
    if has_next:
        g2_ref, x_out, h_out = rest
    else:
        x_out, = rest
    sub = a_ref.shape[0] // n_sub
    for s in range(n_sub):
        rows = pl.ds(s * sub, sub)
        xn = res_ref[rows, :] + _rms(_dot(a_ref[rows, :], w_ref[...]), g1_ref[...])
        x_out[rows, :] = xn
        if has_next:
            h_out[rows, :] = _rms(xn, g2_ref[...]).astype(h_out.dtype)


def mm_norm_res(a, w, layer, res, g1, g2=None, tm=512, n_sub=4):
    m, kdim = a.shape
    d = w.shape[2]
    has_next = g2 is not None
    row = pl.BlockSpec((tm, d), lambda i: (i, 0))
    vec = pl.BlockSpec((1, d), lambda i: (0, 0))
    in_specs = [pl.BlockSpec((tm, kdim), lambda i: (i, 0)),
                pl.BlockSpec((None, kdim, d), lambda i: (layer, 0, 0), pipeline_mode=pl.Buffered(1)), row, vec]
    args = [a, w, res, g1.reshape(1, d)]
    out_specs, out_shape = [row], [jax.ShapeDtypeStruct((m, d), F32)]
    if has_next:
        in_specs.append(vec)
        args.append(g2.reshape(1, d))
        out_specs.append(row)
        out_shape.append(jax.ShapeDtypeStruct((m, d), BF16))
    out = pl.pallas_call(
        functools.partial(_mm_norm_res_kernel, n_sub=n_sub, has_next=has_next),
        grid=(m // tm,),
        in_specs=in_specs,
        out_specs=out_specs,
        out_shape=out_shape,
        compiler_params=_params(1),
        name="mm_norm_res",
    )(*args)
    return tuple(out) if has_next else (out[0], None)


def _ffn_up_kernel(*refs, tiles_per_seq, tail_frac, n_side):
    h_ref, wg_ref, wu_ref, cg_ref, cu_ref, bg_ref, bu_ref = refs[:7]
    side_in, (o_ref, *side_out) = refs[7:7 + n_side], refs[7 + n_side:8 + 2 * n_side]
    wg_bf, wu_bf, carry_g, carry_u = refs[8 + 2 * n_side:]
    _side_cast(side_in, side_out)
    _cast_weights_once([(wg_ref, wg_bf), (wu_ref, wu_bf)])
    first = pl.program_id(1) % tiles_per_seq == 0
    cg = jnp.where(first, 0.0, carry_g[...])
    cu = jnp.where(first, 0.0, carry_u[...])
    tm = h_ref.shape[0]
    cut = tm - tm // tail_frac
    pg = _dot(h_ref[...], wg_bf[...])
    pu_a = _dot(h_ref[0:cut, :], wu_bf[...])
    pu_b = _dot(h_ref[cut:tm, :], wu_bf[...])
    gate = _causal_conv3(pg, cg, cg_ref[...]) + bg_ref[...]
    sg = gate * _sigmoid(gate)
    up_a = _causal_conv3(pu_a, cu, cu_ref[...]) + bu_ref[...]
    o_ref[0:cut, :] = (sg[0:cut, :] * up_a).astype(o_ref.dtype)
    up_b = _causal_conv3(pu_b, pu_a[cut - 8:, :], cu_ref[...]) + bu_ref[...]
    o_ref[cut:tm, :] = (sg[cut:tm, :] * up_b).astype(o_ref.dtype)
    carry_g[...] = pg[tm - 8:, :]
    carry_u[...] = pu_b[tm - cut - 8:, :]


def ffn_up(h, w_up, layer, conv_w, conv_b, seq, side=(), tm=1024, tn=512, tail_frac=4):
    m, d = h.shape
    f = w_up.shape[2] // 2
    nj = f // tn
    conv_b = conv_b.reshape(1, 2 * f)
    side_in, side_out, side_shapes = _side_cast_specs(side, nj * (m // tm), m // tm)
    wspec = lambda off: pl.BlockSpec((None, d, tn), lambda j, i: (layer, 0, off * nj + j))
    return pl.pallas_call(
        functools.partial(_ffn_up_kernel, tiles_per_seq=seq // tm, tail_frac=tail_frac, n_side=len(side)),
        grid=(nj, m // tm),
        in_specs=[
            pl.BlockSpec((tm, d), lambda j, i: (i, 0)),
            wspec(0), wspec(1),
            pl.BlockSpec((3, tn), lambda j, i: (0, j)),
            pl.BlockSpec((3, tn), lambda j, i: (0, nj + j)),
            pl.BlockSpec((1, tn), lambda j, i: (0, j)),
            pl.BlockSpec((1, tn), lambda j, i: (0, nj + j)),
        ] + side_in,
        out_specs=[pl.BlockSpec((tm, tn), lambda j, i: (i, j))] + side_out,
        out_shape=[jax.ShapeDtypeStruct((m, f), BF16)] + side_shapes,
        scratch_shapes=[pltpu.VMEM((d, tn), BF16)] * 2 + [pltpu.VMEM((8, tn), F32)] * 2,
        compiler_params=_params(2),
        name="ffn_up",
    )(h, w_up, w_up, conv_w, conv_w, conv_b, conv_b, *[w for w, _ in side])


def _token_shift_delta(h_ref, halo_ref, tiles_per_seq, axis):
    h = h_ref[...].astype(F32)
    prev = halo_ref[BF16_SUBLANES - 1:BF16_SUBLANES, :].astype(F32)
    prev = jnp.where(pl.program_id(axis) % tiles_per_seq == 0, 0.0, prev)
    return h, _shift_rows(h, prev) - h


def _rw_lora_kernel(h_ref, halo_ref, mu_ref, w1_ref, a1_ref, g1_ref, tw_ref, ta_ref, tg_ref, *, tiles_per_seq):
    h, xx = _token_shift_delta(h_ref, halo_ref, tiles_per_seq, 0)
    xw = (h + xx * mu_ref[1:2, :]).astype(BF16)
    xa = (h + xx * mu_ref[4:5, :]).astype(BF16)
    xg = (h + xx * mu_ref[5:6, :]).astype(BF16)
    tw_ref[...] = jnp.tanh(_dot(xw, w1_ref[...])).astype(tw_ref.dtype)
    ta_ref[...] = _dot(xa, a1_ref[...]).astype(ta_ref.dtype)
    tg_ref[...] = _sigmoid(_dot(xg, g1_ref[...])).astype(tg_ref.dtype)


def _halo_spec(tm, d, grid_axis):
    step = tm // BF16_SUBLANES
    if grid_axis == 0:
        return pl.BlockSpec((BF16_SUBLANES, d), lambda i: (jnp.maximum(i * step - 1, 0), 0))
    return pl.BlockSpec((BF16_SUBLANES, d), lambda j, i: (jnp.maximum(i * step - 1, 0), 0))


def rwkv_lora(h, mu, w1, a1, g1, seq, tm=512):
    m, d = h.shape
    full = lambda arr: pl.BlockSpec(arr.shape, lambda i: (0, 0))
    outs = [w1.shape[1], a1.shape[1], g1.shape[1]]
    return pl.pallas_call(
        functools.partial(_rw_lora_kernel, tiles_per_seq=seq // tm),
        grid=(m // tm,),
        in_specs=[pl.BlockSpec((tm, d), lambda i: (i, 0)), _halo_spec(tm, d, 0), full(mu), full(w1), full(a1), full(g1)],
        out_specs=[pl.BlockSpec((tm, n), lambda i: (i, 0)) for n in outs],
        out_shape=[jax.ShapeDtypeStruct((m, n), BF16) for n in outs],
        compiler_params=_params(1),
        name="rwkv_lora",
    )(h, h, mu, w1, a1, g1)


def _rw_proj_kernel(h_ref, halo_ref, mu_ref, wr_ref, wk_ref, wv_ref, tw_ref, ta_ref, tg_ref,
                    w2_ref, a2_ref, g2_ref, w0_ref, a0_ref, kk_ref, ka_ref, bd_ref,
                    r_out, ld_out, k_out, v_out, av_out, bv_out, g_out, *, tiles_per_seq, n_sub):
    h_all, xx_all = _token_shift_delta(h_ref, halo_ref, tiles_per_seq, 1)
    sub = h_ref.shape[0] // n_sub
    for s in range(n_sub):
        rows = pl.ds(s * sub, sub)
        h = h_all[s * sub:(s + 1) * sub, :]
        xx = xx_all[s * sub:(s + 1) * sub, :]
        xr = (h + xx * mu_ref[0:1, :]).astype(BF16)
        xk = (h + xx * mu_ref[2:3, :]).astype(BF16)
        xv = (h + xx * mu_ref[3:4, :]).astype(BF16)
        r = _dot(xr, wr_ref[...])
        k = _dot(xk, wk_ref[...])
        v = _dot(xv, wv_ref[...])
        wl = w0_ref[...] + _dot(tw_ref[rows, :], w2_ref[...])
        ld = -DECAY_SCALE * _sigmoid(wl)
        a = _sigmoid(a0_ref[...] + _dot(ta_ref[rows, :], a2_ref[...]))
        g = _dot(tg_ref[rows, :], g2_ref[...])
        kk = k * kk_ref[...]
        nrm = jnp.sqrt(_head_sum(kk * kk, bd_ref[...]))
        kk = kk / jnp.maximum(nrm, L2_EPS)
        k = k * (1.0 + (a - 1.0) * ka_ref[...])
        r_out[rows, :] = r.astype(r_out.dtype)
        ld_out[rows, :] = ld
        k_out[rows, :] = k.astype(k_out.dtype)
        v_out[rows, :] = v.astype(v_out.dtype)
        av_out[rows, :] = (-kk).astype(av_out.dtype)
        bv_out[rows, :] = (kk * a).astype(bv_out.dtype)
        g_out[rows, :] = g.astype(g_out.dtype)


def rwkv_proj(h, mu, wr, wk, wv, tw, ta, tg, w2, a2, g2, w0, a0, k_k, k_a, bd, seq, tm=256, n_sub=1):
    m, d = h.shape
    tn = d
    colw = lambda arr: pl.BlockSpec((arr.shape[0], tn), lambda j, i: (0, j), pipeline_mode=pl.Buffered(1))
    rowt = lambda arr: pl.BlockSpec((tm, arr.shape[1]), lambda j, i: (i, 0))
    vecs = [w0.reshape(1, d), a0.reshape(1, d), k_k.reshape(1, d), k_a.reshape(1, d)]
    out_dtypes = [BF16, F32, BF16, BF16, BF16, BF16, BF16]
    return pl.pallas_call(
        functools.partial(_rw_proj_kernel, tiles_per_seq=seq // tm, n_sub=n_sub),
        grid=(d // tn, m // tm),
        in_specs=[pl.BlockSpec((tm, d), lambda j, i: (i, 0)), _halo_spec(tm, d, 1),
                  pl.BlockSpec(mu.shape, lambda j, i: (0, 0)),
                  colw(wr), colw(wk), colw(wv), rowt(tw), rowt(ta), rowt(tg), colw(w2), colw(a2), colw(g2)]
                 + [colw(x) for x in vecs] + [pl.BlockSpec(bd.shape, lambda j, i: (0, 0))],
        out_specs=[pl.BlockSpec((tm, tn), lambda j, i: (i, j)) for _ in out_dtypes],
        out_shape=[jax.ShapeDtypeStruct((m, d), dt) for dt in out_dtypes],
        compiler_params=_params(2),
        name="rwkv_proj",
    )(h, h, mu, wr, wk, wv, tw, ta, tg, w2, a2, g2, *vecs, bd)


def _stack_pair(x, blk):
    return jnp.where(blk, jnp.concatenate([x, x], axis=0), 0.0).astype(BF16)


def _rw_scan_kernel(r_ref, ld_ref, k_ref, v_ref, av_ref, bv_ref, g_ref, rk_ref, gng_ref, gnb_ref, bd_ref,
                    o_ref, s_ref, *, n_pairs):
    L = CHUNK

    @pl.when(pl.program_id(2) == 0)
    def _():
        s_ref[...] = jnp.zeros_like(s_ref)

    n2 = 2 * L
    row = lax.broadcasted_iota(jnp.int32, (n2, n2), 0)
    col = lax.broadcasted_iota(jnp.int32, (n2, n2), 1)
    same = (row // L) == (col // L)
    strict = same & (row > col)
    incl = same & (row >= col)
    eye = (row == col).astype(F32)
    tr = lax.broadcasted_iota(jnp.int32, (L, 4 * L), 0)
    tc = lax.broadcasted_iota(jnp.int32, (L, 4 * L), 1)
    tri = ((tr >= tc % L) & (tc < 3 * L)).astype(BF16)
    bd = bd_ref[...]

    pairs = range(n_pairs)
    cols = [slice(p * PAIR, (p + 1) * PAIR) for p in pairs]
    cat0 = lambda xs: jnp.concatenate(xs, axis=0)

    def state_free(rows):
        ld = ld_ref[rows, :]
        hi, mid, lo = _split3(ld)
        cum = _dot(tri, jnp.concatenate([hi, mid, lo, jnp.zeros_like(lo)], axis=0))
        e_pos = jnp.exp(cum)
        e_neg = jnp.exp(-cum)
        r = r_ref[rows, :].astype(F32)
        k = k_ref[rows, :].astype(F32)
        v = v_ref[rows, :].astype(F32)
        rt_w = r * e_pos
        kt_w = k * e_neg
        bt_w = bv_ref[rows, :].astype(F32) * e_neg
        at_w = av_ref[rows, :].astype(F32) * jnp.exp(cum - ld)
        rt = [_stack_pair(rt_w[:, c], same) for c in cols]
        kt = [_stack_pair(kt_w[:, c], same) for c in cols]
        bt = [_stack_pair(bt_w[:, c], same) for c in cols]
        at = [_stack_pair(at_w[:, c], same) for c in cols]
        vs = [_stack_pair(v[:, c], same) for c in cols]

        ar = [cat0([at[p], rt[p]]) for p in pairs]
        bk = [cat0([bt[p], kt[p]]) for p in pairs]
        big = [_dot_nt(ar[p], bk[p]) for p in pairs]
        a_ab = [jnp.where(strict, x[:n2, :n2], 0.0) for x in big]
        a_ak = [jnp.where(strict, x[:n2, n2:], 0.0).astype(BF16) for x in big]
        a_rb = [jnp.where(incl, x[n2:, :n2], 0.0).astype(BF16) for x in big]
        a_rk = [jnp.where(incl, x[n2:, n2:], 0.0).astype(BF16) for x in big]

        inv = [eye + a for a in a_ab]
        ab = [a.astype(BF16) for a in a_ab]
        pw = [_dot(x, x) for x in ab]
        for _ in range(L.bit_length() - 3):
            pwb = [x.astype(BF16) for x in pw]
            both = [_dot(cat0([inv[p].astype(BF16), pwb[p]]), pwb[p]) for p in pairs]
            inv = [inv[p] + both[p][:n2, :] for p in pairs]
            pw = [both[p][n2:, :] for p in pairs]
        invb = [(inv[p] + _dot(inv[p].astype(BF16), pw[p].astype(BF16))).astype(BF16) for p in pairs]
        akv = [_dot(cat0([a_ak[p], a_rk[p]]), vs[p]) for p in pairs]
        bonus = _head_sum(r * k * rk_ref[...], bd) * v
        return dict(ar=ar, bk=bk, vs=vs, a_rb=a_rb, invb=invb, akv=akv, bonus=bonus, w_last=e_pos[L - 1:L, :])

    def advance(c, s):
        sb = [x.astype(BF16) for x in s]
        ars = [_dot_nt(c["ar"][p], sb[p]) for p in pairs]
        ub = [_dot(c["invb"][p], (ars[p][:n2, :] + c["akv"][p][:n2, :]).astype(BF16)).astype(BF16) for p in pairs]
        y2 = [ars[p][n2:, :] + c["akv"][p][n2:, :] + _dot(c["a_rb"][p], ub[p]) for p in pairs]
        s = [(s[p] + _dot_tn(cat0([ub[p], c["vs"][p]]), c["bk"][p])) * c["w_last"][:, cols[p]] for p in pairs]
        return jnp.concatenate([x[:L, :] + x[L:, :] for x in y2], axis=1), s

    def finish(rows, c, y):
        mean = _head_sum(y, bd) * (1.0 / HEAD_SIZE)
        yc = y - mean
        var = _head_sum(yc * yc, bd) * (1.0 / HEAD_SIZE)
        yn = yc * lax.rsqrt(var + GN_EPS) * gng_ref[...] + gnb_ref[...]
        o_ref[rows, :] = ((yn + c["bonus"]) * g_ref[rows, :].astype(F32)).astype(o_ref.dtype)

    chunk_rows = [pl.ds(i * L, L) for i in range(r_ref.shape[0] // L)]
    free = [state_free(rows) for rows in chunk_rows]
    s = [s_ref[p] for p in pairs]
    for rows, c in zip(chunk_rows, free):
        y, s = advance(c, s)
        finish(rows, c, y)
    for p in pairs:
        s_ref[p] = s[p]


def rwkv_scan(r, ld, k, v, av, bv, g, r_k, gn_g, gn_b, bd, seq, n_pairs=16, chunks_per_step=2):
    m, d = r.shape
    n_pairs = min(n_pairs, d // PAIR)
    cw = n_pairs * PAIR
    nc = seq // (CHUNK * chunks_per_step)
    blk = pl.BlockSpec((CHUNK * chunks_per_step, cw), lambda b, j, c: (b * nc + c, j))
    vec = pl.BlockSpec((1, cw), lambda b, j, c: (0, j))
    return pl.pallas_call(
        functools.partial(_rw_scan_kernel, n_pairs=n_pairs),
        grid=(m // seq, d // cw, nc),
        in_specs=[blk] * 7 + [vec] * 3 + [pl.BlockSpec(bd.shape, lambda b, j, c: (0, 0))],
        out_specs=blk,
        out_shape=jax.ShapeDtypeStruct((m, d), BF16),
        scratch_shapes=[pltpu.VMEM((n_pairs, PAIR, PAIR), F32)],
        compiler_params=_params(3),
        name="rwkv_scan",
    )(r, ld, k, v, av, bv, g, r_k.reshape(1, d), gn_g.reshape(1, d), gn_b.reshape(1, d), bd)


def _pad_cols(w, n):
    return jnp.pad(w, ((0, 0), (0, n - w.shape[1])))


def _pad_rows(w, n):
    return jnp.pad(w, ((0, n - w.shape[0]), (0, 0)))


def kernel(x, norm_g, sc_w_in, sc_conv, sc_w_out, rw_mu, rw_wr, rw_wk, rw_wv, rw_wo, rw_w0, rw_w1, rw_w2,
           rw_a0, rw_a1, rw_a2, rw_g1, rw_g2, rw_kk, rw_ka, rw_rk, rw_gn_g, rw_gn_b,
           ffn_w_up, ffn_conv, ffn_conv_b, ffn_w_down):
    bsz, seq, d = x.shape
    depth = norm_g.shape[0]
    xf = x.reshape(bsz * seq, d)
    lane = jnp.arange(LANES)
    bd = (lane[:, None] // HEAD_SIZE == lane[None, :] // HEAD_SIZE).astype(BF16)
    bf = lambda w: w.astype(BF16)
    bf16_w = {}

    def weights_needed_after(i):
        need = [("down", i, ffn_w_down)]
        if i + 1 < depth:
            j = (i + 1) // 2
            if (i + 1) % 2 == 0:
                need.append(("sc_out", j, sc_w_out))
            else:
                need += [("wr", j, rw_wr), ("wk", j, rw_wk), ("wv", j, rw_wv), ("wo", j, rw_wo)]
        return need

    def run_with_side(fn, need):
        out, *copies = fn([(w, layer) for _, layer, w in need])
        for (name, layer, _), c in zip(need, copies):
            bf16_w[name, layer] = c[None]
        return out

    h = norm_cast(xf, norm_g[0, 0])
    for i in range(depth):
        j = i // 2
        if i % 2 == 0:
            mix = run_with_side(lambda side: shortconv_in(h, sc_w_in, j, sc_conv[j], seq, side),
                                [("sc_out", j, sc_w_out)] if i == 0 else [])
            w_mix_out = bf16_w["sc_out", j]
        else:
            lora = LANES * pl.cdiv(rw_w1.shape[2], LANES)
            tw, ta, tg = rwkv_lora(h, rw_mu[j], bf(_pad_cols(rw_w1[j], lora)), bf(_pad_cols(rw_a1[j], lora)),
                                   bf(rw_g1[j]), seq)
            parts = rwkv_proj(h, rw_mu[j], bf16_w["wr", j][0], bf16_w["wk", j][0], bf16_w["wv", j][0], tw, ta, tg,
                              bf(_pad_rows(rw_w2[j], lora)), bf(_pad_rows(rw_a2[j], lora)), bf(rw_g2[j]),
                              rw_w0[j], rw_a0[j], rw_kk[j], rw_ka[j], bd, seq)
            mix = rwkv_scan(*parts, rw_rk[j], rw_gn_g[j], rw_gn_b[j], bd, seq)
            w_mix_out = bf16_w["wo", j]
        xf, h = mm_norm_res(mix, w_mix_out, 0, xf, norm_g[i, 1], norm_g[i, 2], tm=512, n_sub=4)
        act = run_with_side(lambda side: ffn_up(h, ffn_w_up, i, ffn_conv[i], ffn_conv_b[i], seq, side),
                            weights_needed_after(i))
        g_next = norm_g[i + 1, 0] if i + 1 < depth else None
        xf, h = mm_norm_res(act, bf16_w["down", i], 0, xf, norm_g[i, 3], g_next, tm=256, n_sub=2)
    return xf.reshape(bsz, seq, d)
```

```python
import functools
import math

import jax
import jax.numpy as jnp
from jax import lax
from jax.experimental import pallas as pl
from jax.experimental.pallas import tpu as pltpu

F32 = jnp.float32
BF16 = jnp.bfloat16

HEAD_SIZE = 64
RMS_EPS = 1e-6
GN_EPS = 64e-5
L2_EPS = 1e-12
DECAY_SCALE = math.exp(-0.5)

LANES = 128
BF16_SUBLANES = 16
VMEM_LIMIT_BYTES = 56 * 1024 * 1024
CHUNK = 64
PAIR = 2 * HEAD_SIZE


def _params(n_axes):
    return pltpu.CompilerParams(dimension_semantics=("arbitrary",) * n_axes,
                                vmem_limit_bytes=VMEM_LIMIT_BYTES)


def _dot(a, b):
    return jnp.dot(a, b, preferred_element_type=F32)


def _dot_nt(a, b):
    return lax.dot_general(a, b, (((1,), (1,)), ((), ())), preferred_element_type=F32)


def _dot_tn(a, b):
    return lax.dot_general(a, b, (((0,), (0,)), ((), ())), preferred_element_type=F32)


def _split2(x):
    hi = x.astype(BF16)
    lo = (x - hi.astype(F32)).astype(BF16)
    return hi, lo


def _split3(x):
    hi = x.astype(BF16)
    r1 = x - hi.astype(F32)
    mid = r1.astype(BF16)
    lo = (r1 - mid.astype(F32)).astype(BF16)
    return hi, mid, lo


def _rms(y, g):
    return y * lax.rsqrt(jnp.mean(y * y, axis=-1, keepdims=True) + RMS_EPS) * g


def _sigmoid(x):
    return 1.0 / (1.0 + jnp.exp(-x))


def _head_sum(x, bd):
    rows, n = x.shape[0], x.shape[1] // LANES
    if n > 1:
        x = jnp.concatenate([x[:, s * LANES:(s + 1) * LANES] for s in range(n)], axis=0)
    hi, lo = _split2(x)
    out = _dot(jnp.concatenate([hi, lo], axis=1), jnp.concatenate([bd, bd], axis=0))
    if n > 1:
        out = jnp.concatenate([out[s * rows:(s + 1) * rows, :] for s in range(n)], axis=1)
    return out


def _shift_rows(p, prev1, prev2=None):
    row = lax.broadcasted_iota(jnp.int32, p.shape, 0)
    p1 = jnp.where(row == 0, prev1, pltpu.roll(p, 1, 0))
    if prev2 is None:
        return p1
    p2 = jnp.where(row == 0, prev2, jnp.where(row == 1, prev1, pltpu.roll(p, 2, 0)))
    return p1, p2


def _causal_conv3(p, carry, w):
    p1, p2 = _shift_rows(p, carry[7:8, :], carry[6:7, :])
    return w[0:1, :] * p2 + w[1:2, :] * p1 + w[2:3, :] * p


def _norm_cast_kernel(x_ref, g_ref, o_ref):
    o_ref[...] = _rms(x_ref[...], g_ref[...]).astype(o_ref.dtype)


def norm_cast(x, g, tm=512):
    m, d = x.shape
    return pl.pallas_call(
        _norm_cast_kernel,
        grid=(m // tm,),
        in_specs=[pl.BlockSpec((tm, d), lambda i: (i, 0)), pl.BlockSpec((1, d), lambda i: (0, 0))],
        out_specs=pl.BlockSpec((tm, d), lambda i: (i, 0)),
        out_shape=jax.ShapeDtypeStruct((m, d), BF16),
        compiler_params=_params(1),
        name="norm_cast",
    )(x, g.reshape(1, d))


def _side_cast_specs(side, n_steps, n_inner):
    in_specs, out_specs, out_shapes = [], [], []
    for w, layer in side:
        _, rows, cols = w.shape
        blk = BF16_SUBLANES
        while rows % blk or rows // blk > n_steps:
            blk *= 2
        last = rows // blk - 1
        in_specs.append(pl.BlockSpec(
            (None, blk, cols), lambda j, i, layer=layer, last=last: (layer, jnp.minimum(j * n_inner + i, last), 0)))
        out_specs.append(pl.BlockSpec((blk, cols), lambda j, i, last=last: (jnp.minimum(j * n_inner + i, last), 0)))
        out_shapes.append(jax.ShapeDtypeStruct((rows, cols), BF16))
    return in_specs, out_specs, out_shapes


def _side_cast(src_refs, dst_refs):
    for src_ref, dst_ref in zip(src_refs, dst_refs):
        dst_ref[...] = src_ref[...].astype(dst_ref.dtype)


def _cast_weights_once(pairs):
    @pl.when(pl.program_id(1) == 0)
    def _():
        for src_ref, dst_ref in pairs:
            dst_ref[...] = src_ref[...].astype(dst_ref.dtype)


def _sc_in_kernel(*refs, tiles_per_seq, n_sub, n_side):
    h_ref, wb_ref, wc_ref, wh_ref, cw_ref = refs[:5]
    side_in, (o_ref, *side_out) = refs[5:5 + n_side], refs[5 + n_side:6 + 2 * n_side]
    wb_bf, wc_bf, wh_bf, carry_ref = refs[6 + 2 * n_side:]
    _side_cast(side_in, side_out)
    _cast_weights_once([(wb_ref, wb_bf), (wc_ref, wc_bf), (wh_ref, wh_bf)])
    carry = jnp.where(pl.program_id(1) % tiles_per_seq == 0, 0.0, carry_ref[...])
    sub = h_ref.shape[0] // n_sub
    cw = cw_ref[...]
    for s in range(n_sub):
        rows = pl.ds(s * sub, sub)
        h = h_ref[rows, :]
        p = _dot(h, wc_bf[...]) * _dot(h, wh_bf[...])
        y = _causal_conv3(p, carry, cw)
        carry = p[sub - 8:, :]
        o_ref[rows, :] = (_dot(h, wb_bf[...]) * y).astype(o_ref.dtype)
    carry_ref[...] = carry


def shortconv_in(h, w_in, layer, conv_w, seq, side=(), tm=1024, tn=512, n_sub=1):
    m, d = h.shape
    nj = d // tn
    side_in, side_out, side_shapes = _side_cast_specs(side, nj * (m // tm), m // tm)
    wspec = lambda off: pl.BlockSpec((None, d, tn), lambda j, i: (layer, 0, off * nj + j))
    return pl.pallas_call(
        functools.partial(_sc_in_kernel, tiles_per_seq=seq // tm, n_sub=n_sub, n_side=len(side)),
        grid=(nj, m // tm),
        in_specs=[
            pl.BlockSpec((tm, d), lambda j, i: (i, 0)),
            wspec(0), wspec(1), wspec(2),
            pl.BlockSpec((3, tn), lambda j, i: (0, j)),
        ] + side_in,
        out_specs=[pl.BlockSpec((tm, tn), lambda j, i: (i, j))] + side_out,
        out_shape=[jax.ShapeDtypeStruct((m, d), BF16)] + side_shapes,
        scratch_shapes=[pltpu.VMEM((d, tn), BF16)] * 3 + [pltpu.VMEM((8, tn), F32)],
        compiler_params=_params(2),
        name="shortconv_in",
    )(h, w_in, w_in, w_in, conv_w, *[w for w, _ in side])


def _mm_norm_res_kernel(a_ref, w_ref, res_ref, g1_ref, *rest, n_sub, has_next):
    if has_next:
        g2_ref, x_out, h_out = rest
    else:
        x_out, = rest
    sub = a_ref.shape[0] // n_sub
    for s in range(n_sub):
        rows = pl.ds(s * sub, sub)
        xn = res_ref[rows, :] + _rms(_dot(a_ref[rows, :], w_ref[...]), g1_ref[...])
        x_out[rows, :] = xn
        if has_next:
            h_out[rows, :] = _rms(xn, g2_ref[...]).astype(h_out.dtype)


def mm_norm_res(a, w, layer, res, g1, g2=None, tm=512, n_sub=4):
    m, kdim = a.shape
    d = w.shape[2]
    has_next = g2 is not None
    row = pl.BlockSpec((tm, d), lambda i: (i, 0))
    vec = pl.BlockSpec((1, d), lambda i: (0, 0))
    in_specs = [pl.BlockSpec((tm, kdim), lambda i: (i, 0)),
                pl.BlockSpec((None, kdim, d), lambda i: (layer, 0, 0), pipeline_mode=pl.Buffered(1)), row, vec]
    args = [a, w, res, g1.reshape(1, d)]
    out_specs, out_shape = [row], [jax.ShapeDtypeStruct((m, d), F32)]
    if has_next:
        in_specs.append(vec)
        args.append(g2.reshape(1, d))
        out_specs.append(row)
        out_shape.append(jax.ShapeDtypeStruct((m, d), BF16))
    out = pl.pallas_call(
        functools.partial(_mm_norm_res_kernel, n_sub=n_sub, has_next=has_next),
        grid=(m // tm,),
        in_specs=in_specs,
        out_specs=out_specs,
        out_shape=out_shape,
        compiler_params=_params(1),
        name="mm_norm_res",
    )(*args)
    return tuple(out) if has_next else (out[0], None)


def _ffn_up_kernel(*refs, tiles_per_seq, tail_frac, n_side):
    h_ref, wg_ref, wu_ref, cg_ref, cu_ref, bg_ref, bu_ref = refs[:7]
    side_in, (o_ref, *side_out) = refs[7:7 + n_side], refs[7 + n_side:8 + 2 * n_side]
    wg_bf, wu_bf, carry_g, carry_u = refs[8 + 2 * n_side:]
    _side_cast(side_in, side_out)
    _cast_weights_once([(wg_ref, wg_bf), (wu_ref, wu_bf)])
    first = pl.program_id(1) % tiles_per_seq == 0
    cg = jnp.where(first, 0.0, carry_g[...])
    cu = jnp.where(first, 0.0, carry_u[...])
    tm = h_ref.shape[0]
    cut = tm - tm // tail_frac
    pg_a = _dot(h_ref[0:cut, :], wg_bf[...])
    pg_b = _dot(h_ref[cut:tm, :], wg_bf[...])
    pu_a = _dot(h_ref[0:cut, :], wu_bf[...])
    pu_b = _dot(h_ref[cut:tm, :], wu_bf[...])
    gate_a = _causal_conv3(pg_a, cg, cg_ref[...]) + bg_ref[...]
    gate_b = _causal_conv3(pg_b, pg_a[cut - 8:, :], cg_ref[...]) + bg_ref[...]
    up_a = _causal_conv3(pu_a, cu, cu_ref[...]) + bu_ref[...]
    o_ref[0:cut, :] = (gate_a * _sigmoid(gate_a) * up_a).astype(o_ref.dtype)
    up_b = _causal_conv3(pu_b, pu_a[cut - 8:, :], cu_ref[...]) + bu_ref[...]
    o_ref[cut:tm, :] = (gate_b * _sigmoid(gate_b) * up_b).astype(o_ref.dtype)
    carry_g[...] = pg_b[tm - cut - 8:, :]
    carry_u[...] = pu_b[tm - cut - 8:, :]


def ffn_up(h, w_up, layer, conv_w, conv_b, seq, side=(), tm=1024, tn=512, tail_frac=4):
    m, d = h.shape
    f = w_up.shape[2] // 2
    nj = f // tn
    conv_b = conv_b.reshape(1, 2 * f)
    side_in, side_out, side_shapes = _side_cast_specs(side, nj * (m // tm), m // tm)
    wspec = lambda off: pl.BlockSpec((None, d, tn), lambda j, i: (layer, 0, off * nj + j))
    return pl.pallas_call(
        functools.partial(_ffn_up_kernel, tiles_per_seq=seq // tm, tail_frac=tail_frac, n_side=len(side)),
        grid=(nj, m // tm),
        in_specs=[
            pl.BlockSpec((tm, d), lambda j, i: (i, 0)),
            wspec(0), wspec(1),
            pl.BlockSpec((3, tn), lambda j, i: (0, j)),
            pl.BlockSpec((3, tn), lambda j, i: (0, nj + j)),
            pl.BlockSpec((1, tn), lambda j, i: (0, j)),
            pl.BlockSpec((1, tn), lambda j, i: (0, nj + j)),
        ] + side_in,
        out_specs=[pl.BlockSpec((tm, tn), lambda j, i: (i, j))] + side_out,
        out_shape=[jax.ShapeDtypeStruct((m, f), BF16)] + side_shapes,
        scratch_shapes=[pltpu.VMEM((d, tn), BF16)] * 2 + [pltpu.VMEM((8, tn), F32)] * 2,
        compiler_params=_params(2),
        name="ffn_up",
    )(h, w_up, w_up, conv_w, conv_w, conv_b, conv_b, *[w for w, _ in side])


def _token_shift_delta(h_ref, halo_ref, tiles_per_seq, axis):
    h = h_ref[...].astype(F32)
    prev = halo_ref[BF16_SUBLANES - 1:BF16_SUBLANES, :].astype(F32)
    prev = jnp.where(pl.program_id(axis) % tiles_per_seq == 0, 0.0, prev)
    return h, _shift_rows(h, prev) - h


def _rw_lora_kernel(h_ref, halo_ref, mu_ref, w1_ref, a1_ref, g1_ref, tw_ref, ta_ref, tg_ref, *, tiles_per_seq):
    h, xx = _token_shift_delta(h_ref, halo_ref, tiles_per_seq, 0)
    xw = (h + xx * mu_ref[1:2, :]).astype(BF16)
    xa = (h + xx * mu_ref[4:5, :]).astype(BF16)
    xg = (h + xx * mu_ref[5:6, :]).astype(BF16)
    tw_ref[...] = jnp.tanh(_dot(xw, w1_ref[...])).astype(tw_ref.dtype)
    ta_ref[...] = _dot(xa, a1_ref[...]).astype(ta_ref.dtype)
    tg_ref[...] = _sigmoid(_dot(xg, g1_ref[...])).astype(tg_ref.dtype)


def _halo_spec(tm, d, grid_axis):
    step = tm // BF16_SUBLANES
    if grid_axis == 0:
        return pl.BlockSpec((BF16_SUBLANES, d), lambda i: (jnp.maximum(i * step - 1, 0), 0))
    return pl.BlockSpec((BF16_SUBLANES, d), lambda j, i: (jnp.maximum(i * step - 1, 0), 0))


def rwkv_lora(h, mu, w1, a1, g1, seq, tm=512):
    m, d = h.shape
    full = lambda arr: pl.BlockSpec(arr.shape, lambda i: (0, 0))
    outs = [w1.shape[1], a1.shape[1], g1.shape[1]]
    return pl.pallas_call(
        functools.partial(_rw_lora_kernel, tiles_per_seq=seq // tm),
        grid=(m // tm,),
        in_specs=[pl.BlockSpec((tm, d), lambda i: (i, 0)), _halo_spec(tm, d, 0), full(mu), full(w1), full(a1), full(g1)],
        out_specs=[pl.BlockSpec((tm, n), lambda i: (i, 0)) for n in outs],
        out_shape=[jax.ShapeDtypeStruct((m, n), BF16) for n in outs],
        compiler_params=_params(1),
        name="rwkv_lora",
    )(h, h, mu, w1, a1, g1)


def _rw_proj_kernel(h_ref, halo_ref, mu_ref, wr_ref, wk_ref, wv_ref, tw_ref, ta_ref, tg_ref,
                    w2_ref, a2_ref, g2_ref, w0_ref, a0_ref, kk_ref, ka_ref, bd_ref,
                    r_out, ld_out, k_out, v_out, av_out, bv_out, g_out, *, tiles_per_seq, n_sub):
    h_all, xx_all = _token_shift_delta(h_ref, halo_ref, tiles_per_seq, 1)
    sub = h_ref.shape[0] // n_sub
    for s in range(n_sub):
        rows = pl.ds(s * sub, sub)
        h = h_all[s * sub:(s + 1) * sub, :]
        xx = xx_all[s * sub:(s + 1) * sub, :]
        xk = (h + xx * mu_ref[2:3, :]).astype(BF16)
        xr = (h + xx * mu_ref[0:1, :]).astype(BF16)
        xv = (h + xx * mu_ref[3:4, :]).astype(BF16)
        k = _dot(xk, wk_ref[...])
        wl = w0_ref[...] + _dot(tw_ref[rows, :], w2_ref[...])
        a = _sigmoid(a0_ref[...] + _dot(ta_ref[rows, :], a2_ref[...]))
        g_out[rows, :] = _dot(tg_ref[rows, :], g2_ref[...]).astype(g_out.dtype)
        kk = k * kk_ref[...]
        nrm = jnp.sqrt(_head_sum(kk * kk, bd_ref[...]))
        r_out[rows, :] = _dot(xr, wr_ref[...]).astype(r_out.dtype)
        v_out[rows, :] = _dot(xv, wv_ref[...]).astype(v_out.dtype)
        ld_out[rows, :] = -DECAY_SCALE * _sigmoid(wl)
        kk = kk / jnp.maximum(nrm, L2_EPS)
        k_out[rows, :] = (k * (1.0 + (a - 1.0) * ka_ref[...])).astype(k_out.dtype)
        av_out[rows, :] = (-kk).astype(av_out.dtype)
        bv_out[rows, :] = (kk * a).astype(bv_out.dtype)


def rwkv_proj(h, mu, wr, wk, wv, tw, ta, tg, w2, a2, g2, w0, a0, k_k, k_a, bd, seq, tm=256, n_sub=1):
    m, d = h.shape
    tn = d
    colw = lambda arr: pl.BlockSpec((arr.shape[0], tn), lambda j, i: (0, j), pipeline_mode=pl.Buffered(1))
    rowt = lambda arr: pl.BlockSpec((tm, arr.shape[1]), lambda j, i: (i, 0))
    vecs = [w0.reshape(1, d), a0.reshape(1, d), k_k.reshape(1, d), k_a.reshape(1, d)]
    out_dtypes = [BF16, F32, BF16, BF16, BF16, BF16, BF16]
    return pl.pallas_call(
        functools.partial(_rw_proj_kernel, tiles_per_seq=seq // tm, n_sub=n_sub),
        grid=(d // tn, m // tm),
        in_specs=[pl.BlockSpec((tm, d), lambda j, i: (i, 0)), _halo_spec(tm, d, 1),
                  pl.BlockSpec(mu.shape, lambda j, i: (0, 0)),
                  colw(wr), colw(wk), colw(wv), rowt(tw), rowt(ta), rowt(tg), colw(w2), colw(a2), colw(g2)]
                 + [colw(x) for x in vecs] + [pl.BlockSpec(bd.shape, lambda j, i: (0, 0))],
        out_specs=[pl.BlockSpec((tm, tn), lambda j, i: (i, j)) for _ in out_dtypes],
        out_shape=[jax.ShapeDtypeStruct((m, d), dt) for dt in out_dtypes],
        compiler_params=_params(2),
        name="rwkv_proj",
    )(h, h, mu, wr, wk, wv, tw, ta, tg, w2, a2, g2, *vecs, bd)


def _stack_pair(x, blk):
    return jnp.where(blk, jnp.concatenate([x, x], axis=0), 0.0).astype(BF16)


def _rw_scan_kernel(r_ref, ld_ref, k_ref, v_ref, av_ref, bv_ref, g_ref, rk_ref, gng_ref, gnb_ref, bd_ref,
                    o_ref, s_ref, *, n_pairs):
    L = CHUNK

    @pl.when(pl.program_id(2) == 0)
    def _():
        s_ref[...] = jnp.zeros_like(s_ref)

    n2 = 2 * L
    row = lax.broadcasted_iota(jnp.int32, (n2, n2), 0)
    col = lax.broadcasted_iota(jnp.int32, (n2, n2), 1)
    same = (row // L) == (col // L)
    strict = same & (row > col)
    incl = same & (row >= col)
    eye = (row == col).astype(F32)
    tr = lax.broadcasted_iota(jnp.int32, (L, 4 * L), 0)
    tc = lax.broadcasted_iota(jnp.int32, (L, 4 * L), 1)
    tri = ((tr >= tc % L) & (tc < 3 * L)).astype(BF16)
    bd = bd_ref[...]

    pairs = range(n_pairs)
    cols = [slice(p * PAIR, (p + 1) * PAIR) for p in pairs]
    cat0 = lambda xs: jnp.concatenate(xs, axis=0)

    def state_free(rows):
        ld = ld_ref[rows, :]
        hi, mid, lo = _split3(ld)
        cum = _dot(tri, jnp.concatenate([hi, mid, lo, jnp.zeros_like(lo)], axis=0))
        e_pos = jnp.exp(cum)
        e_neg = jnp.exp(-cum)
        r = r_ref[rows, :].astype(F32)
        k = k_ref[rows, :].astype(F32)
        v = v_ref[rows, :].astype(F32)
        rt_w = r * e_pos
        kt_w = k * e_neg
        bt_w = bv_ref[rows, :].astype(F32) * e_neg
        at_w = av_ref[rows, :].astype(F32) * jnp.exp(cum - ld)
        rt = [_stack_pair(rt_w[:, c], same) for c in cols]
        kt = [_stack_pair(kt_w[:, c], same) for c in cols]
        bt = [_stack_pair(bt_w[:, c], same) for c in cols]
        at = [_stack_pair(at_w[:, c], same) for c in cols]
        vs = [_stack_pair(v[:, c], same) for c in cols]

        ar = [cat0([at[p], rt[p]]) for p in pairs]
        bk = [cat0([bt[p], kt[p]]) for p in pairs]
        big = [_dot_nt(ar[p], bk[p]) for p in pairs]
        a_ab = [jnp.where(strict, x[:n2, :n2], 0.0) for x in big]
        a_ak = [jnp.where(strict, x[:n2, n2:], 0.0).astype(BF16) for x in big]
        a_rb = [jnp.where(incl, x[n2:, :n2], 0.0).astype(BF16) for x in big]
        a_rk = [jnp.where(incl, x[n2:, n2:], 0.0).astype(BF16) for x in big]

        inv = [eye + a for a in a_ab]
        ab = [a.astype(BF16) for a in a_ab]
        pw = [_dot(x, x) for x in ab]
        for _ in range(L.bit_length() - 3):
            pwb = [x.astype(BF16) for x in pw]
            both = [_dot(cat0([inv[p].astype(BF16), pwb[p]]), pwb[p]) for p in pairs]
            inv = [inv[p] + both[p][:n2, :] for p in pairs]
            pw = [both[p][n2:, :] for p in pairs]
        invb = [(inv[p] + _dot(inv[p].astype(BF16), pw[p].astype(BF16))).astype(BF16) for p in pairs]
        akv = [_dot(cat0([a_ak[p], a_rk[p]]), vs[p]) for p in pairs]
        bonus = _head_sum(r * k * rk_ref[...], bd) * v
        return dict(ar=ar, bk=bk, vs=vs, a_rb=a_rb, invb=invb, akv=akv, bonus=bonus, w_last=e_pos[L - 1:L, :])

    def advance(c, s):
        sb = [x.astype(BF16) for x in s]
        ars = [_dot_nt(c["ar"][p], sb[p]) for p in pairs]
        ub = [_dot(c["invb"][p], (ars[p][:n2, :] + c["akv"][p][:n2, :]).astype(BF16)).astype(BF16) for p in pairs]
        y2 = [ars[p][n2:, :] + c["akv"][p][n2:, :] + _dot(c["a_rb"][p], ub[p]) for p in pairs]
        s = [(s[p] + _dot_tn(cat0([ub[p], c["vs"][p]]), c["bk"][p])) * c["w_last"][:, cols[p]] for p in pairs]
        return jnp.concatenate([x[:L, :] + x[L:, :] for x in y2], axis=1), s

    def finish(rows, c, y):
        mean = _head_sum(y, bd) * (1.0 / HEAD_SIZE)
        yc = y - mean
        var = _head_sum(yc * yc, bd) * (1.0 / HEAD_SIZE)
        yn = yc * lax.rsqrt(var + GN_EPS) * gng_ref[...] + gnb_ref[...]
        o_ref[rows, :] = ((yn + c["bonus"]) * g_ref[rows, :].astype(F32)).astype(o_ref.dtype)

    chunk_rows = [pl.ds(i * L, L) for i in range(r_ref.shape[0] // L)]
    free = [state_free(rows) for rows in chunk_rows]
    s = [s_ref[p] for p in pairs]
    for rows, c in zip(chunk_rows, free):
        y, s = advance(c, s)
        finish(rows, c, y)
    for p in pairs:
        s_ref[p] = s[p]


def rwkv_scan(r, ld, k, v, av, bv, g, r_k, gn_g, gn_b, bd, seq, n_pairs=16, chunks_per_step=2):
    m, d = r.shape
    n_pairs = min(n_pairs, d // PAIR)
    cw = n_pairs * PAIR
    nc = seq // (CHUNK * chunks_per_step)
    blk = pl.BlockSpec((CHUNK * chunks_per_step, cw), lambda b, j, c: (b * nc + c, j))
    vec = pl.BlockSpec((1, cw), lambda b, j, c: (0, j))
    return pl.pallas_call(
        functools.partial(_rw_scan_kernel, n_pairs=n_pairs),
        grid=(m // seq, d // cw, nc),
        in_specs=[blk] * 7 + [vec] * 3 + [pl.BlockSpec(bd.shape, lambda b, j, c: (0, 0))],
        out_specs=blk,
        out_shape=jax.ShapeDtypeStruct((m, d), BF16),
        scratch_shapes=[pltpu.VMEM((n_pairs, PAIR, PAIR), F32)],
        compiler_params=_params(3),
        name="rwkv_scan",
    )(r, ld, k, v, av, bv, g, r_k.reshape(1, d), gn_g.reshape(1, d), gn_b.reshape(1, d), bd)


def _pad_cols(w, n):
    return jnp.pad(w, ((0, 0), (0, n - w.shape[1])))


def _pad_rows(w, n):
    return jnp.pad(w, ((0, n - w.shape[0]), (0, 0)))


def kernel(x, norm_g, sc_w_in, sc_conv, sc_w_out, rw_mu, rw_wr, rw_wk, rw_wv, rw_wo, rw_w0, rw_w1, rw_w2,
           rw_a0, rw_a1, rw_a2, rw_g1, rw_g2, rw_kk, rw_ka, rw_rk, rw_gn_g, rw_gn_b,
           ffn_w_up, ffn_conv, ffn_conv_b, ffn_w_down):
    bsz, seq, d = x.shape
    depth = norm_g.shape[0]
    xf = x.reshape(bsz * seq, d)
    lane = jnp.arange(LANES)
    bd = (lane[:, None] // HEAD_SIZE == lane[None, :] // HEAD_SIZE).astype(BF16)
    bf = lambda w: w.astype(BF16)
    bf16_w = {}

    def weights_needed_after(i):
        need = [("down", i, ffn_w_down)]
        if i + 1 < depth:
            j = (i + 1) // 2
            if (i + 1) % 2 == 0:
                need.append(("sc_out", j, sc_w_out))
            else:
                need += [("wr", j, rw_wr), ("wk", j, rw_wk), ("wv", j, rw_wv), ("wo", j, rw_wo)]
        return need

    def run_with_side(fn, need):
        out, *copies = fn([(w, layer) for _, layer, w in need])
        for (name, layer, _), c in zip(need, copies):
            bf16_w[name, layer] = c[None]
        return out

    h = norm_cast(xf, norm_g[0, 0])
    for i in range(depth):
        j = i // 2
        if i % 2 == 0:
            mix = run_with_side(lambda side: shortconv_in(h, sc_w_in, j, sc_conv[j], seq, side),
                                [("sc_out", j, sc_w_out)] if i == 0 else [])
            w_mix_out = bf16_w["sc_out", j]
        else:
            lora = LANES * pl.cdiv(rw_w1.shape[2], LANES)
            tw, ta, tg = rwkv_lora(h, rw_mu[j], bf(_pad_cols(rw_w1[j], lora)), bf(_pad_cols(rw_a1[j], lora)),
                                   bf(rw_g1[j]), seq)
            parts = rwkv_proj(h, rw_mu[j], bf16_w["wr", j][0], bf16_w["wk", j][0], bf16_w["wv", j][0], tw, ta, tg,
                              bf(_pad_rows(rw_w2[j], lora)), bf(_pad_rows(rw_a2[j], lora)), bf(rw_g2[j]),
                              rw_w0[j], rw_a0[j], rw_kk[j], rw_ka[j], bd, seq)
            mix = rwkv_scan(*parts, rw_rk[j], rw_gn_g[j], rw_gn_b[j], bd, seq)
            w_mix_out = bf16_w["wo", j]
        xf, h = mm_norm_res(mix, w_mix_out, 0, xf, norm_g[i, 1], norm_g[i, 2], tm=512, n_sub=4)
        act = run_with_side(lambda side: ffn_up(h, ffn_w_up, i, ffn_conv[i], ffn_conv_b[i], seq, side),
                            weights_needed_after(i))
        g_next = norm_g[i + 1, 0] if i + 1 < depth else None
        xf, h = mm_norm_res(act, bf16_w["down", i], 0, xf, norm_g[i, 3], g_next, tm=256, n_sub=2)
    return xf.reshape(bsz, seq, d)
```

```python
import functools
import math

import jax
import jax.numpy as jnp
from jax import lax
from jax.experimental import pallas as pl
from jax.experimental.pallas import tpu as pltpu

F32 = jnp.float32
BF16 = jnp.bfloat16

HEAD_SIZE = 64
RMS_EPS = 1e-6
GN_EPS = 64e-5
L2_EPS = 1e-12
DECAY_SCALE = math.exp(-0.5)

LANES = 128
BF16_SUBLANES = 16
VMEM_LIMIT_BYTES = 56 * 1024 * 1024
CHUNK = 64
PAIR = 2 * HEAD_SIZE


def _params(n_axes):
    return pltpu.CompilerParams(dimension_semantics=("arbitrary",) * n_axes,
                                vmem_limit_bytes=VMEM_LIMIT_BYTES)


def _dot(a, b):
    return jnp.dot(a, b, preferred_element_type=F32)


def _dot_nt(a, b):
    return lax.dot_general(a, b, (((1,), (1,)), ((), ())), preferred_element_type=F32)


def _dot_tn(a, b):
    return lax.dot_general(a, b, (((0,), (0,)), ((), ())), preferred_element_type=F32)


def _split2(x):
    hi = x.astype(BF16)
    lo = (x - hi.astype(F32)).astype(BF16)
    return hi, lo


def _split3(x):
    hi = x.astype(BF16)
    r1 = x - hi.astype(F32)
    mid = r1.astype(BF16)
    lo = (r1 - mid.astype(F32)).astype(BF16)
    return hi, mid, lo


def _rms(y, g):
    return y * lax.rsqrt(jnp.mean(y * y, axis=-1, keepdims=True) + RMS_EPS) * g


def _sigmoid(x):
    return 1.0 / (1.0 + jnp.exp(-x))


def _head_sum(x, bd):
    rows, n = x.shape[0], x.shape[1] // LANES
    if n > 1:
        x = jnp.concatenate([x[:, s * LANES:(s + 1) * LANES] for s in range(n)], axis=0)
    hi, lo = _split2(x)
    out = _dot(jnp.concatenate([hi, lo], axis=1), jnp.concatenate([bd, bd], axis=0))
    if n > 1:
        out = jnp.concatenate([out[s * rows:(s + 1) * rows, :] for s in range(n)], axis=1)
    return out


def _shift_rows(p, prev1, prev2=None):
    row = lax.broadcasted_iota(jnp.int32, p.shape, 0)
    p1 = jnp.where(row == 0, prev1, pltpu.roll(p, 1, 0))
    if prev2 is None:
        return p1
    p2 = jnp.where(row == 0, prev2, jnp.where(row == 1, prev1, pltpu.roll(p, 2, 0)))
    return p1, p2


def _causal_conv3(p, carry, w):
    p1, p2 = _shift_rows(p, carry[7:8, :], carry[6:7, :])
    return w[0:1, :] * p2 + w[1:2, :] * p1 + w[2:3, :] * p


def _norm_cast_kernel(x_ref, g_ref, o_ref):
    o_ref[...] = _rms(x_ref[...], g_ref[...]).astype(o_ref.dtype)


def norm_cast(x, g, tm=512):
    m, d = x.shape
    return pl.pallas_call(
        _norm_cast_kernel,
        grid=(m // tm,),
        in_specs=[pl.BlockSpec((tm, d), lambda i: (i, 0)), pl.BlockSpec((1, d), lambda i: (0, 0))],
        out_specs=pl.BlockSpec((tm, d), lambda i: (i, 0)),
        out_shape=jax.ShapeDtypeStruct((m, d), BF16),
        compiler_params=_params(1),
        name="norm_cast",
    )(x, g.reshape(1, d))


def _side_cast_specs(side, n_steps, n_inner):
    in_specs, out_specs, out_shapes = [], [], []
    for w, layer in side:
        _, rows, cols = w.shape
        blk = BF16_SUBLANES
        while rows % blk or rows // blk > n_steps:
            blk *= 2
        last = rows // blk - 1
        in_specs.append(pl.BlockSpec(
            (None, blk, cols), lambda j, i, layer=layer, last=last: (layer, jnp.minimum(j * n_inner + i, last), 0)))
        out_specs.append(pl.BlockSpec((blk, cols), lambda j, i, last=last: (jnp.minimum(j * n_inner + i, last), 0)))
        out_shapes.append(jax.ShapeDtypeStruct((rows, cols), BF16))
    return in_specs, out_specs, out_shapes


def _side_cast(src_refs, dst_refs):
    for src_ref, dst_ref in zip(src_refs, dst_refs):
        dst_ref[...] = src_ref[...].astype(dst_ref.dtype)


def _cast_weights_once(pairs):
    @pl.when(pl.program_id(1) == 0)
    def _():
        for src_ref, dst_ref in pairs:
            dst_ref[...] = src_ref[...].astype(dst_ref.dtype)


def _sc_in_kernel(*refs, tiles_per_seq, n_sub, n_side):
    h_ref, wb_ref, wc_ref, wh_ref, cw_ref = refs[:5]
    side_in, (o_ref, *side_out) = refs[5:5 + n_side], refs[5 + n_side:6 + 2 * n_side]
    wb_bf, wc_bf, wh_bf, carry_ref = refs[6 + 2 * n_side:]
    _side_cast(side_in, side_out)
    _cast_weights_once([(wb_ref, wb_bf), (wc_ref, wc_bf), (wh_ref, wh_bf)])
    carry = jnp.where(pl.program_id(1) % tiles_per_seq == 0, 0.0, carry_ref[...])
    sub = h_ref.shape[0] // n_sub
    cw = cw_ref[...]
    for s in range(n_sub):
        rows = pl.ds(s * sub, sub)
        h = h_ref[rows, :]
        p = _dot(h, wc_bf[...]) * _dot(h, wh_bf[...])
        y = _causal_conv3(p, carry, cw)
        carry = p[sub - 8:, :]
        o_ref[rows, :] = (_dot(h, wb_bf[...]) * y).astype(o_ref.dtype)
    carry_ref[...] = carry


def shortconv_in(h, w_in, layer, conv_w, seq, side=(), tm=1024, tn=512, n_sub=1):
    m, d = h.shape
    nj = d // tn
    side_in, side_out, side_shapes = _side_cast_specs(side, nj * (m // tm), m // tm)
    wspec = lambda off: pl.BlockSpec((None, d, tn), lambda j, i: (layer, 0, off * nj + j))
    return pl.pallas_call(
        functools.partial(_sc_in_kernel, tiles_per_seq=seq // tm, n_sub=n_sub, n_side=len(side)),
        grid=(nj, m // tm),
        in_specs=[
            pl.BlockSpec((tm, d), lambda j, i: (i, 0)),
            wspec(0), wspec(1), wspec(2),
            pl.BlockSpec((3, tn), lambda j, i: (0, j)),
        ] + side_in,
        out_specs=[pl.BlockSpec((tm, tn), lambda j, i: (i, j))] + side_out,
        out_shape=[jax.ShapeDtypeStruct((m, d), BF16)] + side_shapes,
        scratch_shapes=[pltpu.VMEM((d, tn), BF16)] * 3 + [pltpu.VMEM((8, tn), F32)],
        compiler_params=_params(2),
        name="shortconv_in",
    )(h, w_in, w_in, w_in, conv_w, *[w for w, _ in side])


def _mm_norm_res_kernel(a_ref, w_ref, res_ref, g1_ref, *rest, n_sub, has_next):
    if has_next:
        g2_ref, x_out, h_out = rest
    else:
        x_out, = rest
    sub = a_ref.shape[0] // n_sub
    for s in range(n_sub):
        rows = pl.ds(s * sub, sub)
        xn = res_ref[rows, :] + _rms(_dot(a_ref[rows, :], w_ref[...]), g1_ref[...])
        x_out[rows, :] = xn
        if has_next:
            h_out[rows, :] = _rms(xn, g2_ref[...]).astype(h_out.dtype)


def mm_norm_res(a, w, layer, res, g1, g2=None, tm=512, n_sub=4):
    m, kdim = a.shape
    d = w.shape[2]
    has_next = g2 is not None
    row = pl.BlockSpec((tm, d), lambda i: (i, 0))
    vec = pl.BlockSpec((1, d), lambda i: (0, 0))
    in_specs = [pl.BlockSpec((tm, kdim), lambda i: (i, 0)),
                pl.BlockSpec((None, kdim, d), lambda i: (layer, 0, 0), pipeline_mode=pl.Buffered(1)), row, vec]
    args = [a, w, res, g1.reshape(1, d)]
    out_specs, out_shape = [row], [jax.ShapeDtypeStruct((m, d), F32)]
    if has_next:
        in_specs.append(vec)
        args.append(g2.reshape(1, d))
        out_specs.append(row)
        out_shape.append(jax.ShapeDtypeStruct((m, d), BF16))
    out = pl.pallas_call(
        functools.partial(_mm_norm_res_kernel, n_sub=n_sub, has_next=has_next),
        grid=(m // tm,),
        in_specs=in_specs,
        out_specs=out_specs,
        out_shape=out_shape,
        compiler_params=_params(1),
        name="mm_norm_res",
    )(*args)
    return tuple(out) if has_next else (out[0], None)


def _ffn_up_kernel(*refs, tiles_per_seq, tail_frac, n_side):
    h_ref, wg_ref, wu_ref, cg_ref, cu_ref, bg_ref, bu_ref = refs[:7]
    side_in, (o_ref, *side_out) = refs[7:7 + n_side], refs[7 + n_side:8 + 2 * n_side]
    wg_bf, wu_bf, carry_g, carry_u = refs[8 + 2 * n_side:]
    _side_cast(side_in, side_out)
    _cast_weights_once([(wg_ref, wg_bf), (wu_ref, wu_bf)])
    first = pl.program_id(1) % tiles_per_seq == 0
    cg = jnp.where(first, 0.0, carry_g[...])
    cu = jnp.where(first, 0.0, carry_u[...])
    tm = h_ref.shape[0]
    cut = tm - tm // tail_frac
    pg_a = _dot(h_ref[0:cut, :], wg_bf[...])
    pg_b = _dot(h_ref[cut:tm, :], wg_bf[...])
    pu_a = _dot(h_ref[0:cut, :], wu_bf[...])
    pu_b = _dot(h_ref[cut:tm, :], wu_bf[...])
    gate_a = _causal_conv3(pg_a, cg, cg_ref[...]) + bg_ref[...]
    gate_b = _causal_conv3(pg_b, pg_a[cut - 8:, :], cg_ref[...]) + bg_ref[...]
    up_a = _causal_conv3(pu_a, cu, cu_ref[...]) + bu_ref[...]
    o_ref[0:cut, :] = (gate_a * _sigmoid(gate_a) * up_a).astype(o_ref.dtype)
    up_b = _causal_conv3(pu_b, pu_a[cut - 8:, :], cu_ref[...]) + bu_ref[...]
    o_ref[cut:tm, :] = (gate_b * _sigmoid(gate_b) * up_b).astype(o_ref.dtype)
    carry_g[...] = pg_b[tm - cut - 8:, :]
    carry_u[...] = pu_b[tm - cut - 8:, :]


def ffn_up(h, w_up, layer, conv_w, conv_b, seq, side=(), tm=1024, tn=512, tail_frac=4):
    m, d = h.shape
    f = w_up.shape[2] // 2
    nj = f // tn
    conv_b = conv_b.reshape(1, 2 * f)
    side_in, side_out, side_shapes = _side_cast_specs(side, nj * (m // tm), m // tm)
    wspec = lambda off: pl.BlockSpec((None, d, tn), lambda j, i: (layer, 0, off * nj + j))
    return pl.pallas_call(
        functools.partial(_ffn_up_kernel, tiles_per_seq=seq // tm, tail_frac=tail_frac, n_side=len(side)),
        grid=(nj, m // tm),
        in_specs=[
            pl.BlockSpec((tm, d), lambda j, i: (i, 0)),
            wspec(0), wspec(1),
            pl.BlockSpec((3, tn), lambda j, i: (0, j)),
            pl.BlockSpec((3, tn), lambda j, i: (0, nj + j)),
            pl.BlockSpec((1, tn), lambda j, i: (0, j)),
            pl.BlockSpec((1, tn), lambda j, i: (0, nj + j)),
        ] + side_in,
        out_specs=[pl.BlockSpec((tm, tn), lambda j, i: (i, j))] + side_out,
        out_shape=[jax.ShapeDtypeStruct((m, f), BF16)] + side_shapes,
        scratch_shapes=[pltpu.VMEM((d, tn), BF16)] * 2 + [pltpu.VMEM((8, tn), F32)] * 2,
        compiler_params=_params(2),
        name="ffn_up",
    )(h, w_up, w_up, conv_w, conv_w, conv_b, conv_b, *[w for w, _ in side])


def _token_shift_delta(h_ref, halo_ref, tiles_per_seq, axis):
    h = h_ref[...].astype(F32)
    prev = halo_ref[BF16_SUBLANES - 1:BF16_SUBLANES, :].astype(F32)
    prev = jnp.where(pl.program_id(axis) % tiles_per_seq == 0, 0.0, prev)
    return h, _shift_rows(h, prev) - h


def _rw_lora_kernel(h_ref, halo_ref, mu_ref, w1_ref, a1_ref, g1_ref, tw_ref, ta_ref, tg_ref, *, tiles_per_seq):
    h, xx = _token_shift_delta(h_ref, halo_ref, tiles_per_seq, 0)
    xw = (h + xx * mu_ref[1:2, :]).astype(BF16)
    xa = (h + xx * mu_ref[4:5, :]).astype(BF16)
    xg = (h + xx * mu_ref[5:6, :]).astype(BF16)
    tw_ref[...] = jnp.tanh(_dot(xw, w1_ref[...])).astype(tw_ref.dtype)
    ta_ref[...] = _dot(xa, a1_ref[...]).astype(ta_ref.dtype)
    tg_ref[...] = _sigmoid(_dot(xg, g1_ref[...])).astype(tg_ref.dtype)


def _halo_spec(tm, d, grid_axis):
    step = tm // BF16_SUBLANES
    if grid_axis == 0:
        return pl.BlockSpec((BF16_SUBLANES, d), lambda i: (jnp.maximum(i * step - 1, 0), 0))
    return pl.BlockSpec((BF16_SUBLANES, d), lambda j, i: (jnp.maximum(i * step - 1, 0), 0))


def rwkv_lora(h, mu, w1, a1, g1, seq, tm=512):
    m, d = h.shape
    full = lambda arr: pl.BlockSpec(arr.shape, lambda i: (0, 0))
    outs = [w1.shape[1], a1.shape[1], g1.shape[1]]
    return pl.pallas_call(
        functools.partial(_rw_lora_kernel, tiles_per_seq=seq // tm),
        grid=(m // tm,),
        in_specs=[pl.BlockSpec((tm, d), lambda i: (i, 0)), _halo_spec(tm, d, 0), full(mu), full(w1), full(a1), full(g1)],
        out_specs=[pl.BlockSpec((tm, n), lambda i: (i, 0)) for n in outs],
        out_shape=[jax.ShapeDtypeStruct((m, n), BF16) for n in outs],
        compiler_params=_params(1),
        name="rwkv_lora",
    )(h, h, mu, w1, a1, g1)


def _rw_proj_kernel(h_ref, halo_ref, mu_ref, wr_ref, wk_ref, wv_ref, tw_ref, ta_ref, tg_ref,
                    w2_ref, a2_ref, g2_ref, w0_ref, a0_ref, kk_ref, ka_ref, bd_ref,
                    r_out, ld_out, k_out, v_out, av_out, bv_out, g_out, *, tiles_per_seq, n_sub):
    h_all, xx_all = _token_shift_delta(h_ref, halo_ref, tiles_per_seq, 1)
    sub = h_ref.shape[0] // n_sub
    for s in range(n_sub):
        rows = pl.ds(s * sub, sub)
        h = h_all[s * sub:(s + 1) * sub, :]
        xx = xx_all[s * sub:(s + 1) * sub, :]
        xk = (h + xx * mu_ref[2:3, :]).astype(BF16)
        xr = (h + xx * mu_ref[0:1, :]).astype(BF16)
        xv = (h + xx * mu_ref[3:4, :]).astype(BF16)
        k = _dot(xk, wk_ref[...])
        wl = w0_ref[...] + _dot(tw_ref[rows, :], w2_ref[...])
        a = _sigmoid(a0_ref[...] + _dot(ta_ref[rows, :], a2_ref[...]))
        g_out[rows, :] = _dot(tg_ref[rows, :], g2_ref[...]).astype(g_out.dtype)
        kk = k * kk_ref[...]
        nrm = jnp.sqrt(_head_sum(kk * kk, bd_ref[...]))
        r_out[rows, :] = _dot(xr, wr_ref[...]).astype(r_out.dtype)
        v_out[rows, :] = _dot(xv, wv_ref[...]).astype(v_out.dtype)
        ld_out[rows, :] = -DECAY_SCALE * _sigmoid(wl)
        kk = kk / jnp.maximum(nrm, L2_EPS)
        k_out[rows, :] = (k * (1.0 + (a - 1.0) * ka_ref[...])).astype(k_out.dtype)
        av_out[rows, :] = (-kk).astype(av_out.dtype)
        bv_out[rows, :] = (kk * a).astype(bv_out.dtype)


def rwkv_proj(h, mu, wr, wk, wv, tw, ta, tg, w2, a2, g2, w0, a0, k_k, k_a, bd, seq, tm=256, n_sub=1):
    m, d = h.shape
    tn = d
    colw = lambda arr: pl.BlockSpec((arr.shape[0], tn), lambda j, i: (0, j), pipeline_mode=pl.Buffered(1))
    rowt = lambda arr: pl.BlockSpec((tm, arr.shape[1]), lambda j, i: (i, 0))
    vecs = [w0.reshape(1, d), a0.reshape(1, d), k_k.reshape(1, d), k_a.reshape(1, d)]
    out_dtypes = [BF16, F32, BF16, BF16, BF16, BF16, BF16]
    return pl.pallas_call(
        functools.partial(_rw_proj_kernel, tiles_per_seq=seq // tm, n_sub=n_sub),
        grid=(d // tn, m // tm),
        in_specs=[pl.BlockSpec((tm, d), lambda j, i: (i, 0)), _halo_spec(tm, d, 1),
                  pl.BlockSpec(mu.shape, lambda j, i: (0, 0)),
                  colw(wr), colw(wk), colw(wv), rowt(tw), rowt(ta), rowt(tg), colw(w2), colw(a2), colw(g2)]
                 + [colw(x) for x in vecs] + [pl.BlockSpec(bd.shape, lambda j, i: (0, 0))],
        out_specs=[pl.BlockSpec((tm, tn), lambda j, i: (i, j)) for _ in out_dtypes],
        out_shape=[jax.ShapeDtypeStruct((m, d), dt) for dt in out_dtypes],
        compiler_params=_params(2),
        name="rwkv_proj",
    )(h, h, mu, wr, wk, wv, tw, ta, tg, w2, a2, g2, *vecs, bd)


def _stack_pair(x, blk):
    return jnp.where(blk, jnp.concatenate([x, x], axis=0), 0.0).astype(BF16)


def _rw_scan_kernel(r_ref, ld_ref, k_ref, v_ref, av_ref, bv_ref, g_ref, rk_ref, gng_ref, gnb_ref, bd_ref,
                    o_ref, s_ref, *, n_pairs):
    L = CHUNK

    @pl.when(pl.program_id(2) == 0)
    def _():
        s_ref[...] = jnp.zeros_like(s_ref)

    n2 = 2 * L
    row = lax.broadcasted_iota(jnp.int32, (n2, n2), 0)
    col = lax.broadcasted_iota(jnp.int32, (n2, n2), 1)
    same = (row // L) == (col // L)
    strict = same & (row > col)
    incl = same & (row >= col)
    eye = (row == col).astype(F32)
    tr = lax.broadcasted_iota(jnp.int32, (L, 4 * L), 0)
    tc = lax.broadcasted_iota(jnp.int32, (L, 4 * L), 1)
    tri = ((tr >= tc % L) & (tc < 3 * L)).astype(BF16)
    bd = bd_ref[...]

    pairs = range(n_pairs)
    cols = [slice(p * PAIR, (p + 1) * PAIR) for p in pairs]
    cat0 = lambda xs: jnp.concatenate(xs, axis=0)

    def state_free(rows):
        ld = ld_ref[rows, :]
        hi, mid, lo = _split3(ld)
        cum = _dot(tri, jnp.concatenate([hi, mid, lo, jnp.zeros_like(lo)], axis=0))
        e_pos = jnp.exp(cum)
        e_neg = jnp.exp(-cum)
        r = r_ref[rows, :].astype(F32)
        k = k_ref[rows, :].astype(F32)
        v = v_ref[rows, :].astype(F32)
        rt_w = r * e_pos
        kt_w = k * e_neg
        bt_w = bv_ref[rows, :].astype(F32) * e_neg
        at_w = av_ref[rows, :].astype(F32) * jnp.exp(cum - ld)
        rt = [_stack_pair(rt_w[:, c], same) for c in cols]
        kt = [_stack_pair(kt_w[:, c], same) for c in cols]
        bt = [_stack_pair(bt_w[:, c], same) for c in cols]
        at = [_stack_pair(at_w[:, c], same) for c in cols]
        vs = [_stack_pair(v[:, c], same) for c in cols]

        ar = [cat0([at[p], rt[p]]) for p in pairs]
        bk = [cat0([bt[p], kt[p]]) for p in pairs]
        big = [_dot_nt(ar[p], bk[p]) for p in pairs]
        a_ab = [jnp.where(strict, x[:n2, :n2], 0.0) for x in big]
        a_ak = [jnp.where(strict, x[:n2, n2:], 0.0).astype(BF16) for x in big]
        a_rb = [jnp.where(incl, x[n2:, :n2], 0.0).astype(BF16) for x in big]
        a_rk = [jnp.where(incl, x[n2:, n2:], 0.0).astype(BF16) for x in big]

        inv = [eye + a for a in a_ab]
        ab = [a.astype(BF16) for a in a_ab]
        pw = [_dot(x, x) for x in ab]
        for _ in range(L.bit_length() - 3):
            pwb = [x.astype(BF16) for x in pw]
            both = [_dot(cat0([inv[p].astype(BF16), pwb[p]]), pwb[p]) for p in pairs]
            inv = [inv[p] + both[p][:n2, :] for p in pairs]
            pw = [both[p][n2:, :] for p in pairs]
        invb = [(inv[p] + _dot(inv[p].astype(BF16), pw[p].astype(BF16))).astype(BF16) for p in pairs]
        akv = [_dot(cat0([a_ak[p], a_rk[p]]), vs[p]) for p in pairs]
        bonus = _head_sum(r * k * rk_ref[...], bd) * v
        return dict(ar=ar, bk=bk, vs=vs, a_rb=a_rb, invb=invb, akv=akv, bonus=bonus, w_last=e_pos[L - 1:L, :])

    def advance(c, s):
        sb = [x.astype(BF16) for x in s]
        ars = [_dot_nt(c["ar"][p], sb[p]) for p in pairs]
        ub = [_dot(c["invb"][p], (ars[p][:n2, :] + c["akv"][p][:n2, :]).astype(BF16)).astype(BF16) for p in pairs]
        y2 = [ars[p][n2:, :] + c["akv"][p][n2:, :] + _dot(c["a_rb"][p], ub[p]) for p in pairs]
        s = [(s[p] + _dot_tn(cat0([ub[p], c["vs"][p]]), c["bk"][p])) * c["w_last"][:, cols[p]] for p in pairs]
        return jnp.concatenate([x[:L, :] + x[L:, :] for x in y2], axis=1), s

    def finish(rows, c, y):
        mean = _head_sum(y, bd) * (1.0 / HEAD_SIZE)
        yc = y - mean
        var = _head_sum(yc * yc, bd) * (1.0 / HEAD_SIZE)
        yn = yc * lax.rsqrt(var + GN_EPS) * gng_ref[...] + gnb_ref[...]
        o_ref[rows, :] = ((yn + c["bonus"]) * g_ref[rows, :].astype(F32)).astype(o_ref.dtype)

    chunk_rows = [pl.ds(i * L, L) for i in range(r_ref.shape[0] // L)]
    free = [state_free(rows) for rows in chunk_rows]
    s = [s_ref[p] for p in pairs]
    for rows, c in zip(chunk_rows, free):
        y, s = advance(c, s)
        finish(rows, c, y)
    for p in pairs:
        s_ref[p] = s[p]


def rwkv_scan(r, ld, k, v, av, bv, g, r_k, gn_g, gn_b, bd, seq, n_pairs=16, chunks_per_step=4):
    m, d = r.shape
    n_pairs = min(n_pairs, d // PAIR)
    cw = n_pairs * PAIR
    nc = seq // (CHUNK * chunks_per_step)
    blk = pl.BlockSpec((CHUNK * chunks_per_step, cw), lambda b, j, c: (b * nc + c, j))
    vec = pl.BlockSpec((1, cw), lambda b, j, c: (0, j))
    return pl.pallas_call(
        functools.partial(_rw_scan_kernel, n_pairs=n_pairs),
        grid=(m // seq, d // cw, nc),
        in_specs=[blk] * 7 + [vec] * 3 + [pl.BlockSpec(bd.shape, lambda b, j, c: (0, 0))],
        out_specs=blk,
        out_shape=jax.ShapeDtypeStruct((m, d), BF16),
        scratch_shapes=[pltpu.VMEM((n_pairs, PAIR, PAIR), F32)],
        compiler_params=_params(3),
        name="rwkv_scan",
    )(r, ld, k, v, av, bv, g, r_k.reshape(1, d), gn_g.reshape(1, d), gn_b.reshape(1, d), bd)


def _pad_cols(w, n):
    return jnp.pad(w, ((0, 0), (0, n - w.shape[1])))


def _pad_rows(w, n):
    return jnp.pad(w, ((0, n - w.shape[0]), (0, 0)))


def kernel(x, norm_g, sc_w_in, sc_conv, sc_w_out, rw_mu, rw_wr, rw_wk, rw_wv, rw_wo, rw_w0, rw_w1, rw_w2,
           rw_a0, rw_a1, rw_a2, rw_g1, rw_g2, rw_kk, rw_ka, rw_rk, rw_gn_g, rw_gn_b,
           ffn_w_up, ffn_conv, ffn_conv_b, ffn_w_down):
    bsz, seq, d = x.shape
    depth = norm_g.shape[0]
    xf = x.reshape(bsz * seq, d)
    lane = jnp.arange(LANES)
    bd = (lane[:, None] // HEAD_SIZE == lane[None, :] // HEAD_SIZE).astype(BF16)
    bf = lambda w: w.astype(BF16)
    bf16_w = {}

    def mixer_weights(i):
        j = i // 2
        if i >= depth:
            return []
        if i % 2 == 0:
            return [("sc_out", j, sc_w_out)]
        return [("wr", j, rw_wr), ("wk", j, rw_wk), ("wv", j, rw_wv), ("wo", j, rw_wo)]

    def still_needed(need):
        return [n for n in need if (n[0], n[1]) not in bf16_w]

    def run_with_side(fn, need):
        out, *copies = fn([(w, layer) for _, layer, w in need])
        for (name, layer, _), c in zip(need, copies):
            bf16_w[name, layer] = c[None]
        return out

    h = norm_cast(xf, norm_g[0, 0])
    for i in range(depth):
        j = i // 2
        if i % 2 == 0:
            mix = run_with_side(lambda side: shortconv_in(h, sc_w_in, j, sc_conv[j], seq, side),
                                still_needed(mixer_weights(i) + mixer_weights(i + 1)))
            w_mix_out = bf16_w["sc_out", j]
        else:
            lora = LANES * pl.cdiv(rw_w1.shape[2], LANES)
            tw, ta, tg = rwkv_lora(h, rw_mu[j], bf(_pad_cols(rw_w1[j], lora)), bf(_pad_cols(rw_a1[j], lora)),
                                   bf(rw_g1[j]), seq)
            parts = rwkv_proj(h, rw_mu[j], bf16_w["wr", j][0], bf16_w["wk", j][0], bf16_w["wv", j][0], tw, ta, tg,
                              bf(_pad_rows(rw_w2[j], lora)), bf(_pad_rows(rw_a2[j], lora)), bf(rw_g2[j]),
                              rw_w0[j], rw_a0[j], rw_kk[j], rw_ka[j], bd, seq)
            mix = rwkv_scan(*parts, rw_rk[j], rw_gn_g[j], rw_gn_b[j], bd, seq)
            w_mix_out = bf16_w["wo", j]
        xf, h = mm_norm_res(mix, w_mix_out, 0, xf, norm_g[i, 1], norm_g[i, 2], tm=512, n_sub=4)
        act = run_with_side(lambda side: ffn_up(h, ffn_w_up, i, ffn_conv[i], ffn_conv_b[i], seq, side),
                            still_needed([("down", i, ffn_w_down)] + mixer_weights(i + 1)))
        g_next = norm_g[i + 1, 0] if i + 1 < depth else None
        xf, h = mm_norm_res(act, bf16_w["down", i], 0, xf, norm_g[i, 3], g_next, tm=256, n_sub=2)
    return xf.reshape(bsz, seq, d)
```

```python
import functools
import math

import jax
import jax.numpy as jnp
from jax import lax
from jax.experimental import pallas as pl
from jax.experimental.pallas import tpu as pltpu

F32 = jnp.float32
BF16 = jnp.bfloat16

HEAD_SIZE = 64
RMS_EPS = 1e-6
GN_EPS = 64e-5
L2_EPS = 1e-12
DECAY_SCALE = math.exp(-0.5)

LANES = 128
BF16_SUBLANES = 16
VMEM_LIMIT_BYTES = 56 * 1024 * 1024
PROJ_VMEM_LIMIT_BYTES = 59 * 1024 * 1024
CHUNK = 64
PAIR = 2 * HEAD_SIZE


def _params(n_axes, vmem_limit_bytes=VMEM_LIMIT_BYTES):
    return pltpu.CompilerParams(dimension_semantics=("arbitrary",) * n_axes, vmem_limit_bytes=vmem_limit_bytes)


def _dot(a, b):
    return jnp.dot(a, b, preferred_element_type=F32)


def _dot_nt(a, b):
    return lax.dot_general(a, b, (((1,), (1,)), ((), ())), preferred_element_type=F32)


def _dot_tn(a, b):
    return lax.dot_general(a, b, (((0,), (0,)), ((), ())), preferred_element_type=F32)


def _split2(x):
    hi = x.astype(BF16)
    lo = (x - hi.astype(F32)).astype(BF16)
    return hi, lo


def _split3(x):
    hi = x.astype(BF16)
    r1 = x - hi.astype(F32)
    mid = r1.astype(BF16)
    lo = (r1 - mid.astype(F32)).astype(BF16)
    return hi, mid, lo


def _rms(y, g):
    return y * lax.rsqrt(jnp.mean(y * y, axis=-1, keepdims=True) + RMS_EPS) * g


def _sigmoid(x):
    return 1.0 / (1.0 + jnp.exp(-x))


def _head_sum(x, bd):
    rows, n = x.shape[0], x.shape[1] // LANES
    if n > 1:
        x = jnp.concatenate([x[:, s * LANES:(s + 1) * LANES] for s in range(n)], axis=0)
    hi, lo = _split2(x)
    out = _dot(jnp.concatenate([hi, lo], axis=1), jnp.concatenate([bd, bd], axis=0))
    if n > 1:
        out = jnp.concatenate([out[s * rows:(s + 1) * rows, :] for s in range(n)], axis=1)
    return out


def _shift_rows(p, prev1, prev2=None):
    row = lax.broadcasted_iota(jnp.int32, p.shape, 0)
    p1 = jnp.where(row == 0, prev1, pltpu.roll(p, 1, 0))
    if prev2 is None:
        return p1
    p2 = jnp.where(row == 0, prev2, jnp.where(row == 1, prev1, pltpu.roll(p, 2, 0)))
    return p1, p2


def _causal_conv3(p, carry, w):
    p1, p2 = _shift_rows(p, carry[7:8, :], carry[6:7, :])
    return w[0:1, :] * p2 + w[1:2, :] * p1 + w[2:3, :] * p


def _norm_cast_kernel(x_ref, g_ref, o_ref):
    o_ref[...] = _rms(x_ref[...], g_ref[...]).astype(o_ref.dtype)


def norm_cast(x, g, tm=512):
    m, d = x.shape
    return pl.pallas_call(
        _norm_cast_kernel,
        grid=(m // tm,),
        in_specs=[pl.BlockSpec((tm, d), lambda i: (i, 0)), pl.BlockSpec((1, d), lambda i: (0, 0))],
        out_specs=pl.BlockSpec((tm, d), lambda i: (i, 0)),
        out_shape=jax.ShapeDtypeStruct((m, d), BF16),
        compiler_params=_params(1),
        name="norm_cast",
    )(x, g.reshape(1, d))


def _side_cast_specs(side, n_steps, n_inner):
    in_specs, out_specs, out_shapes = [], [], []
    for w, layer in side:
        _, rows, cols = w.shape
        blk = BF16_SUBLANES
        while rows % blk or rows // blk > n_steps:
            blk *= 2
        last = rows // blk - 1
        in_specs.append(pl.BlockSpec(
            (None, blk, cols), lambda j, i, layer=layer, last=last: (layer, jnp.minimum(j * n_inner + i, last), 0)))
        out_specs.append(pl.BlockSpec((blk, cols), lambda j, i, last=last: (jnp.minimum(j * n_inner + i, last), 0)))
        out_shapes.append(jax.ShapeDtypeStruct((rows, cols), BF16))
    return in_specs, out_specs, out_shapes


def _side_cast(src_refs, dst_refs):
    for src_ref, dst_ref in zip(src_refs, dst_refs):
        dst_ref[...] = src_ref[...].astype(dst_ref.dtype)


def _cast_weights_once(pairs):
    @pl.when(pl.program_id(1) == 0)
    def _():
        for src_ref, dst_ref in pairs:
            dst_ref[...] = src_ref[...].astype(dst_ref.dtype)


def _sc_in_kernel(*refs, tiles_per_seq, n_sub, n_side):
    h_ref, wb_ref, wc_ref, wh_ref, cw_ref = refs[:5]
    side_in, (o_ref, *side_out) = refs[5:5 + n_side], refs[5 + n_side:6 + 2 * n_side]
    wb_bf, wc_bf, wh_bf, carry_ref = refs[6 + 2 * n_side:]
    _side_cast(side_in, side_out)
    _cast_weights_once([(wb_ref, wb_bf), (wc_ref, wc_bf), (wh_ref, wh_bf)])
    carry = jnp.where(pl.program_id(1) % tiles_per_seq == 0, 0.0, carry_ref[...])
    sub = h_ref.shape[0] // n_sub
    cw = cw_ref[...]
    for s in range(n_sub):
        rows = pl.ds(s * sub, sub)
        h = h_ref[rows, :]
        p = _dot(h, wc_bf[...]) * _dot(h, wh_bf[...])
        y = _causal_conv3(p, carry, cw)
        carry = p[sub - 8:, :]
        o_ref[rows, :] = (_dot(h, wb_bf[...]) * y).astype(o_ref.dtype)
    carry_ref[...] = carry


def shortconv_in(h, w_in, layer, conv_w, seq, side=(), tm=1024, tn=512, n_sub=1):
    m, d = h.shape
    nj = d // tn
    side_in, side_out, side_shapes = _side_cast_specs(side, nj * (m // tm), m // tm)
    wspec = lambda off: pl.BlockSpec((None, d, tn), lambda j, i: (layer, 0, off * nj + j))
    return pl.pallas_call(
        functools.partial(_sc_in_kernel, tiles_per_seq=seq // tm, n_sub=n_sub, n_side=len(side)),
        grid=(nj, m // tm),
        in_specs=[
            pl.BlockSpec((tm, d), lambda j, i: (i, 0)),
            wspec(0), wspec(1), wspec(2),
            pl.BlockSpec((3, tn), lambda j, i: (0, j)),
        ] + side_in,
        out_specs=[pl.BlockSpec((tm, tn), lambda j, i: (i, j))] + side_out,
        out_shape=[jax.ShapeDtypeStruct((m, d), BF16)] + side_shapes,
        scratch_shapes=[pltpu.VMEM((d, tn), BF16)] * 3 + [pltpu.VMEM((8, tn), F32)],
        compiler_params=_params(2),
        name="shortconv_in",
    )(h, w_in, w_in, w_in, conv_w, *[w for w, _ in side])


def _mm_norm_res_kernel(a_ref, w_ref, res_ref, g1_ref, *rest, n_sub, has_next):
    if has_next:
        g2_ref, x_out, h_out = rest
    else:
        x_out, = rest
    sub = a_ref.shape[0] // n_sub
    for s in range(n_sub):
        rows = pl.ds(s * sub, sub)
        xn = res_ref[rows, :] + _rms(_dot(a_ref[rows, :], w_ref[...]), g1_ref[...])
        x_out[rows, :] = xn
        if has_next:
            h_out[rows, :] = _rms(xn, g2_ref[...]).astype(h_out.dtype)


def mm_norm_res(a, w, layer, res, g1, g2=None, tm=512, n_sub=4):
    m, kdim = a.shape
    d = w.shape[2]
    has_next = g2 is not None
    row = pl.BlockSpec((tm, d), lambda i: (i, 0))
    vec = pl.BlockSpec((1, d), lambda i: (0, 0))
    in_specs = [pl.BlockSpec((tm, kdim), lambda i: (i, 0)),
                pl.BlockSpec((None, kdim, d), lambda i: (layer, 0, 0), pipeline_mode=pl.Buffered(1)), row, vec]
    args = [a, w, res, g1.reshape(1, d)]
    out_specs, out_shape = [row], [jax.ShapeDtypeStruct((m, d), F32)]
    if has_next:
        in_specs.append(vec)
        args.append(g2.reshape(1, d))
        out_specs.append(row)
        out_shape.append(jax.ShapeDtypeStruct((m, d), BF16))
    out = pl.pallas_call(
        functools.partial(_mm_norm_res_kernel, n_sub=n_sub, has_next=has_next),
        grid=(m // tm,),
        in_specs=in_specs,
        out_specs=out_specs,
        out_shape=out_shape,
        compiler_params=_params(1),
        name="mm_norm_res",
    )(*args)
    return tuple(out) if has_next else (out[0], None)


def _ffn_up_kernel(*refs, tiles_per_seq, tail_frac, n_side):
    h_ref, wg_ref, wu_ref, cg_ref, cu_ref, bg_ref, bu_ref = refs[:7]
    side_in, (o_ref, *side_out) = refs[7:7 + n_side], refs[7 + n_side:8 + 2 * n_side]
    wg_bf, wu_bf, carry_g, carry_u = refs[8 + 2 * n_side:]
    _side_cast(side_in, side_out)
    _cast_weights_once([(wg_ref, wg_bf), (wu_ref, wu_bf)])
    first = pl.program_id(1) % tiles_per_seq == 0
    cg = jnp.where(first, 0.0, carry_g[...])
    cu = jnp.where(first, 0.0, carry_u[...])
    tm = h_ref.shape[0]
    cut = tm - tm // tail_frac
    pg_a = _dot(h_ref[0:cut, :], wg_bf[...])
    pg_b = _dot(h_ref[cut:tm, :], wg_bf[...])
    pu_a = _dot(h_ref[0:cut, :], wu_bf[...])
    pu_b = _dot(h_ref[cut:tm, :], wu_bf[...])
    gate_a = _causal_conv3(pg_a, cg, cg_ref[...]) + bg_ref[...]
    gate_b = _causal_conv3(pg_b, pg_a[cut - 8:, :], cg_ref[...]) + bg_ref[...]
    up_a = _causal_conv3(pu_a, cu, cu_ref[...]) + bu_ref[...]
    o_ref[0:cut, :] = (gate_a * _sigmoid(gate_a) * up_a).astype(o_ref.dtype)
    up_b = _causal_conv3(pu_b, pu_a[cut - 8:, :], cu_ref[...]) + bu_ref[...]
    o_ref[cut:tm, :] = (gate_b * _sigmoid(gate_b) * up_b).astype(o_ref.dtype)
    carry_g[...] = pg_b[tm - cut - 8:, :]
    carry_u[...] = pu_b[tm - cut - 8:, :]


def ffn_up(h, w_up, layer, conv_w, conv_b, seq, side=(), tm=1024, tn=512, tail_frac=4):
    m, d = h.shape
    f = w_up.shape[2] // 2
    nj = f // tn
    conv_b = conv_b.reshape(1, 2 * f)
    side_in, side_out, side_shapes = _side_cast_specs(side, nj * (m // tm), m // tm)
    wspec = lambda off: pl.BlockSpec((None, d, tn), lambda j, i: (layer, 0, off * nj + j))
    return pl.pallas_call(
        functools.partial(_ffn_up_kernel, tiles_per_seq=seq // tm, tail_frac=tail_frac, n_side=len(side)),
        grid=(nj, m // tm),
        in_specs=[
            pl.BlockSpec((tm, d), lambda j, i: (i, 0)),
            wspec(0), wspec(1),
            pl.BlockSpec((3, tn), lambda j, i: (0, j)),
            pl.BlockSpec((3, tn), lambda j, i: (0, nj + j)),
            pl.BlockSpec((1, tn), lambda j, i: (0, j)),
            pl.BlockSpec((1, tn), lambda j, i: (0, nj + j)),
        ] + side_in,
        out_specs=[pl.BlockSpec((tm, tn), lambda j, i: (i, j))] + side_out,
        out_shape=[jax.ShapeDtypeStruct((m, f), BF16)] + side_shapes,
        scratch_shapes=[pltpu.VMEM((d, tn), BF16)] * 2 + [pltpu.VMEM((8, tn), F32)] * 2,
        compiler_params=_params(2),
        name="ffn_up",
    )(h, w_up, w_up, conv_w, conv_w, conv_b, conv_b, *[w for w, _ in side])


def _token_shift_delta(h_ref, halo_ref, tiles_per_seq, axis):
    h = h_ref[...].astype(F32)
    prev = halo_ref[BF16_SUBLANES - 1:BF16_SUBLANES, :].astype(F32)
    prev = jnp.where(pl.program_id(axis) % tiles_per_seq == 0, 0.0, prev)
    return h, _shift_rows(h, prev) - h


def _halo_spec(tm, d, grid_axis):
    step = tm // BF16_SUBLANES
    if grid_axis == 0:
        return pl.BlockSpec((BF16_SUBLANES, d), lambda i: (jnp.maximum(i * step - 1, 0), 0))
    return pl.BlockSpec((BF16_SUBLANES, d), lambda j, i: (jnp.maximum(i * step - 1, 0), 0))


def _rw_proj_kernel(h_ref, halo_ref, mu_ref, wr_ref, wk_ref, wv_ref, w1_ref, a1_ref, g1_ref,
                    w2_ref, a2_ref, g2_ref, w0_ref, a0_ref, kk_ref, ka_ref, bd_ref,
                    r_out, ld_out, k_out, v_out, av_out, bv_out, g_out, *, tiles_per_seq, n_sub):
    h_all, xx_all = _token_shift_delta(h_ref, halo_ref, tiles_per_seq, 1)
    sub = h_ref.shape[0] // n_sub
    for s in range(n_sub):
        rows = pl.ds(s * sub, sub)
        h = h_all[s * sub:(s + 1) * sub, :]
        xx = xx_all[s * sub:(s + 1) * sub, :]
        xk = (h + xx * mu_ref[2:3, :]).astype(BF16)
        xr = (h + xx * mu_ref[0:1, :]).astype(BF16)
        xv = (h + xx * mu_ref[3:4, :]).astype(BF16)
        xw = (h + xx * mu_ref[1:2, :]).astype(BF16)
        xa = (h + xx * mu_ref[4:5, :]).astype(BF16)
        xg = (h + xx * mu_ref[5:6, :]).astype(BF16)
        k = _dot(xk, wk_ref[...])
        tw = jnp.tanh(_dot(xw, w1_ref[...])).astype(BF16)
        ta = _dot(xa, a1_ref[...]).astype(BF16)
        tg = _sigmoid(_dot(xg, g1_ref[...])).astype(BF16)
        wl = w0_ref[...] + _dot(tw, w2_ref[...])
        a = _sigmoid(a0_ref[...] + _dot(ta, a2_ref[...]))
        g_out[rows, :] = _dot(tg, g2_ref[...]).astype(g_out.dtype)
        kk = k * kk_ref[...]
        nrm = jnp.sqrt(_head_sum(kk * kk, bd_ref[...]))
        r_out[rows, :] = _dot(xr, wr_ref[...]).astype(r_out.dtype)
        v_out[rows, :] = _dot(xv, wv_ref[...]).astype(v_out.dtype)
        ld_out[rows, :] = -DECAY_SCALE * _sigmoid(wl)
        kk = kk / jnp.maximum(nrm, L2_EPS)
        k_out[rows, :] = (k * (1.0 + (a - 1.0) * ka_ref[...])).astype(k_out.dtype)
        av_out[rows, :] = (-kk).astype(av_out.dtype)
        bv_out[rows, :] = (kk * a).astype(bv_out.dtype)


def rwkv_proj(h, mu, wr, wk, wv, w1, a1, g1, w2, a2, g2, w0, a0, k_k, k_a, bd, seq, tm=256, n_sub=1):
    m, d = h.shape
    tn = d
    colw = lambda arr: pl.BlockSpec((arr.shape[0], tn), lambda j, i: (0, j), pipeline_mode=pl.Buffered(1))
    whole = lambda arr: pl.BlockSpec(arr.shape, lambda j, i: (0, 0), pipeline_mode=pl.Buffered(1))
    vecs = [w0.reshape(1, d), a0.reshape(1, d), k_k.reshape(1, d), k_a.reshape(1, d)]
    out_dtypes = [BF16, F32, BF16, BF16, BF16, BF16, BF16]
    return pl.pallas_call(
        functools.partial(_rw_proj_kernel, tiles_per_seq=seq // tm, n_sub=n_sub),
        grid=(d // tn, m // tm),
        in_specs=[pl.BlockSpec((tm, d), lambda j, i: (i, 0)), _halo_spec(tm, d, 1),
                  pl.BlockSpec(mu.shape, lambda j, i: (0, 0)),
                  colw(wr), colw(wk), colw(wv), whole(w1), whole(a1), whole(g1), colw(w2), colw(a2), colw(g2)]
                 + [colw(x) for x in vecs] + [pl.BlockSpec(bd.shape, lambda j, i: (0, 0))],
        out_specs=[pl.BlockSpec((tm, tn), lambda j, i: (i, j)) for _ in out_dtypes],
        out_shape=[jax.ShapeDtypeStruct((m, d), dt) for dt in out_dtypes],
        compiler_params=_params(2, PROJ_VMEM_LIMIT_BYTES),
        name="rwkv_proj",
    )(h, h, mu, wr, wk, wv, w1, a1, g1, w2, a2, g2, *vecs, bd)


def _stack_pair(x, blk):
    return jnp.where(blk, jnp.concatenate([x, x], axis=0), 0.0).astype(BF16)


def _rw_scan_kernel(r_ref, ld_ref, k_ref, v_ref, av_ref, bv_ref, g_ref, rk_ref, gng_ref, gnb_ref, bd_ref,
                    o_ref, s_ref, *, n_pairs):
    L = CHUNK

    @pl.when(pl.program_id(2) == 0)
    def _():
        s_ref[...] = jnp.zeros_like(s_ref)

    n2 = 2 * L
    row = lax.broadcasted_iota(jnp.int32, (n2, n2), 0)
    col = lax.broadcasted_iota(jnp.int32, (n2, n2), 1)
    same = (row // L) == (col // L)
    strict = same & (row > col)
    incl = same & (row >= col)
    eye = (row == col).astype(F32)
    tr = lax.broadcasted_iota(jnp.int32, (L, 4 * L), 0)
    tc = lax.broadcasted_iota(jnp.int32, (L, 4 * L), 1)
    tri = ((tr >= tc % L) & (tc < 3 * L)).astype(BF16)
    bd = bd_ref[...]

    pairs = range(n_pairs)
    cols = [slice(p * PAIR, (p + 1) * PAIR) for p in pairs]
    cat0 = lambda xs: jnp.concatenate(xs, axis=0)

    def state_free(rows):
        ld = ld_ref[rows, :]
        hi, mid, lo = _split3(ld)
        cum = _dot(tri, jnp.concatenate([hi, mid, lo, jnp.zeros_like(lo)], axis=0))
        e_pos = jnp.exp(cum)
        e_neg = jnp.exp(-cum)
        r = r_ref[rows, :].astype(F32)
        k = k_ref[rows, :].astype(F32)
        v = v_ref[rows, :].astype(F32)
        rt_w = r * e_pos
        kt_w = k * e_neg
        bt_w = bv_ref[rows, :].astype(F32) * e_neg
        at_w = av_ref[rows, :].astype(F32) * jnp.exp(cum - ld)
        rt = [_stack_pair(rt_w[:, c], same) for c in cols]
        kt = [_stack_pair(kt_w[:, c], same) for c in cols]
        bt = [_stack_pair(bt_w[:, c], same) for c in cols]
        at = [_stack_pair(at_w[:, c], same) for c in cols]
        vs = [_stack_pair(v[:, c], same) for c in cols]

        ar = [cat0([at[p], rt[p]]) for p in pairs]
        bk = [cat0([bt[p], kt[p]]) for p in pairs]
        big = [_dot_nt(ar[p], bk[p]) for p in pairs]
        a_ab = [jnp.where(strict, x[:n2, :n2], 0.0) for x in big]
        a_ak = [jnp.where(strict, x[:n2, n2:], 0.0).astype(BF16) for x in big]
        a_rb = [jnp.where(incl, x[n2:, :n2], 0.0).astype(BF16) for x in big]
        a_rk = [jnp.where(incl, x[n2:, n2:], 0.0).astype(BF16) for x in big]

        inv = [eye + a for a in a_ab]
        ab = [a.astype(BF16) for a in a_ab]
        pw = [_dot(x, x) for x in ab]
        for _ in range(L.bit_length() - 3):
            pwb = [x.astype(BF16) for x in pw]
            both = [_dot(cat0([inv[p].astype(BF16), pwb[p]]), pwb[p]) for p in pairs]
            inv = [inv[p] + both[p][:n2, :] for p in pairs]
            pw = [both[p][n2:, :] for p in pairs]
        invb = [(inv[p] + _dot(inv[p].astype(BF16), pw[p].astype(BF16))).astype(BF16) for p in pairs]
        akv = [_dot(cat0([a_ak[p], a_rk[p]]), vs[p]) for p in pairs]
        bonus = _head_sum(r * k * rk_ref[...], bd) * v
        return dict(ar=ar, bk=bk, vs=vs, a_rb=a_rb, invb=invb, akv=akv, bonus=bonus, w_last=e_pos[L - 1:L, :])

    def advance(c, s):
        sb = [x.astype(BF16) for x in s]
        ars = [_dot_nt(c["ar"][p], sb[p]) for p in pairs]
        ub = [_dot(c["invb"][p], (ars[p][:n2, :] + c["akv"][p][:n2, :]).astype(BF16)).astype(BF16) for p in pairs]
        y2 = [ars[p][n2:, :] + c["akv"][p][n2:, :] + _dot(c["a_rb"][p], ub[p]) for p in pairs]
        s = [(s[p] + _dot_tn(cat0([ub[p], c["vs"][p]]), c["bk"][p])) * c["w_last"][:, cols[p]] for p in pairs]
        return jnp.concatenate([x[:L, :] + x[L:, :] for x in y2], axis=1), s

    def finish(rows, c, y):
        mean = _head_sum(y, bd) * (1.0 / HEAD_SIZE)
        yc = y - mean
        var = _head_sum(yc * yc, bd) * (1.0 / HEAD_SIZE)
        yn = yc * lax.rsqrt(var + GN_EPS) * gng_ref[...] + gnb_ref[...]
        o_ref[rows, :] = ((yn + c["bonus"]) * g_ref[rows, :].astype(F32)).astype(o_ref.dtype)

    chunk_rows = [pl.ds(i * L, L) for i in range(r_ref.shape[0] // L)]
    free = [state_free(rows) for rows in chunk_rows]
    s = [s_ref[p] for p in pairs]
    for rows, c in zip(chunk_rows, free):
        y, s = advance(c, s)
        finish(rows, c, y)
    for p in pairs:
        s_ref[p] = s[p]


def rwkv_scan(r, ld, k, v, av, bv, g, r_k, gn_g, gn_b, bd, seq, n_pairs=16, chunks_per_step=4):
    m, d = r.shape
    n_pairs = min(n_pairs, d // PAIR)
    cw = n_pairs * PAIR
    nc = seq // (CHUNK * chunks_per_step)
    blk = pl.BlockSpec((CHUNK * chunks_per_step, cw), lambda b, j, c: (b * nc + c, j))
    vec = pl.BlockSpec((1, cw), lambda b, j, c: (0, j))
    return pl.pallas_call(
        functools.partial(_rw_scan_kernel, n_pairs=n_pairs),
        grid=(m // seq, d // cw, nc),
        in_specs=[blk] * 7 + [vec] * 3 + [pl.BlockSpec(bd.shape, lambda b, j, c: (0, 0))],
        out_specs=blk,
        out_shape=jax.ShapeDtypeStruct((m, d), BF16),
        scratch_shapes=[pltpu.VMEM((n_pairs, PAIR, PAIR), F32)],
        compiler_params=_params(3),
        name="rwkv_scan",
    )(r, ld, k, v, av, bv, g, r_k.reshape(1, d), gn_g.reshape(1, d), gn_b.reshape(1, d), bd)


def _pad_cols(w, n):
    return jnp.pad(w, ((0, 0), (0, n - w.shape[1])))


def _pad_rows(w, n):
    return jnp.pad(w, ((0, n - w.shape[0]), (0, 0)))


def kernel(x, norm_g, sc_w_in, sc_conv, sc_w_out, rw_mu, rw_wr, rw_wk, rw_wv, rw_wo, rw_w0, rw_w1, rw_w2,
           rw_a0, rw_a1, rw_a2, rw_g1, rw_g2, rw_kk, rw_ka, rw_rk, rw_gn_g, rw_gn_b,
           ffn_w_up, ffn_conv, ffn_conv_b, ffn_w_down):
    bsz, seq, d = x.shape
    depth = norm_g.shape[0]
    xf = x.reshape(bsz * seq, d)
    lane = jnp.arange(LANES)
    bd = (lane[:, None] // HEAD_SIZE == lane[None, :] // HEAD_SIZE).astype(BF16)
    bf = lambda w: w.astype(BF16)
    bf16_w = {}

    def mixer_weights(i):
        j = i // 2
        if i >= depth:
            return []
        if i % 2 == 0:
            return [("sc_out", j, sc_w_out)]
        return [("wr", j, rw_wr), ("wk", j, rw_wk), ("wv", j, rw_wv), ("wo", j, rw_wo)]

    def still_needed(need):
        return [n for n in need if (n[0], n[1]) not in bf16_w]

    def run_with_side(fn, need):
        out, *copies = fn([(w, layer) for _, layer, w in need])
        for (name, layer, _), c in zip(need, copies):
            bf16_w[name, layer] = c[None]
        return out

    h = norm_cast(xf, norm_g[0, 0])
    for i in range(depth):
        j = i // 2
        if i % 2 == 0:
            mix = run_with_side(lambda side: shortconv_in(h, sc_w_in, j, sc_conv[j], seq, side),
                                still_needed(mixer_weights(i) + mixer_weights(i + 1)))
            w_mix_out = bf16_w["sc_out", j]
        else:
            lora = LANES * pl.cdiv(rw_w1.shape[2], LANES)
            parts = rwkv_proj(h, rw_mu[j], bf16_w["wr", j][0], bf16_w["wk", j][0], bf16_w["wv", j][0],
                              bf(_pad_cols(rw_w1[j], lora)), bf(_pad_cols(rw_a1[j], lora)), bf(rw_g1[j]),
                              bf(_pad_rows(rw_w2[j], lora)), bf(_pad_rows(rw_a2[j], lora)), bf(rw_g2[j]),
                              rw_w0[j], rw_a0[j], rw_kk[j], rw_ka[j], bd, seq)
            mix = rwkv_scan(*parts, rw_rk[j], rw_gn_g[j], rw_gn_b[j], bd, seq)
            w_mix_out = bf16_w["wo", j]
        xf, h = mm_norm_res(mix, w_mix_out, 0, xf, norm_g[i, 1], norm_g[i, 2], tm=512, n_sub=4)
        act = run_with_side(lambda side: ffn_up(h, ffn_w_up, i, ffn_conv[i], ffn_conv_b[i], seq, side),
                            still_needed([("down", i, ffn_w_down)] + mixer_weights(i + 1)))
        g_next = norm_g[i + 1, 0] if i + 1 < depth else None
        xf, h = mm_norm_res(act, bf16_w["down", i], 0, xf, norm_g[i, 3], g_next, tm=256, n_sub=2)
    return xf.reshape(bsz, seq, d)
```
